```python
import math
import jax
import jax.numpy as jnp
from jax import lax
import numpy as np

D_MODEL = 1024
BATCH = 8
SEQ = 2048
DEPTH = 1
DEC_BATCH = 128
DEC_SEQ = 4
PAST_LEN = 16384
PAGE_SIZE = 128

MIX_WIDTH = D_MODEL
S5_WIDTH = MIX_WIDTH // 2
S5_GROUP_CH = 16
S5_GROUPS = S5_WIDTH // S5_GROUP_CH
S5_STATE = 64
GLA_WIDTH = MIX_WIDTH - S5_WIDTH
GLA_HEADS = 4
GLA_DV = GLA_WIDTH // GLA_HEADS
GLA_DK = GLA_DV // 2
GLA_QK_WIDTH = GLA_HEADS * GLA_DK
GLA_GATE_RANK = 16
GLA_GATE_NORM = 16.0
GLA_CHUNK = 64
D_FF = 2816
EPS = 1e-6
DT_MIN = 1e-3
DT_MAX = 1e-1

OFF_U = 0
OFF_Q = OFF_U + S5_WIDTH
OFF_K = OFF_Q + GLA_QK_WIDTH
OFF_V = OFF_K + GLA_QK_WIDTH
OFF_G = OFF_V + GLA_WIDTH
OFF_A = OFF_G + GLA_WIDTH
IN_WIDTH = OFF_A + GLA_GATE_RANK

kernel_name = 'hymba_s5_gla_macaron_step'


def rmsnorm(x, g):
    xf = x.astype(jnp.float32)
    y = xf * lax.rsqrt(jnp.mean(xf * xf, axis=-1, keepdims=True) + EPS)
    return (y * g.astype(jnp.float32)).astype(x.dtype)


def swiglu(x, wg, wu, wd):
    return (jax.nn.silu(x @ wg) * (x @ wu)) @ wd


def _cplx_combine(e1, e2):
    a1r, a1i, b1r, b1i = e1
    a2r, a2i, b2r, b2i = e2
    return (a2r * a1r - a2i * a1i,
            a2r * a1i + a2i * a1r,
            a2r * b1r - a2i * b1i + b2r,
            a2r * b1i + a2i * b1r + b2i)


def s5_mixer(u, h0_re, h0_im, lam_re, lam_im, log_dt, b_re, b_im, c_re, c_im, d_skip):
    bsz, t, _ = u.shape
    uf = u.astype(jnp.float32).reshape(bsz, t, S5_GROUPS, S5_GROUP_CH)
    dt = jnp.exp(log_dt.astype(jnp.float32))[:, None]
    lr = lam_re.astype(jnp.float32)
    li = lam_im.astype(jnp.float32)
    mag = jnp.exp(lr * dt)
    abr = mag * jnp.cos(li * dt)
    abi = mag * jnp.sin(li * dt)
    den = lr * lr + li * li
    fr = ((abr - 1.0) * lr + abi * li) / den
    fi = (abi * lr - (abr - 1.0) * li) / den
    br = b_re.astype(jnp.float32)
    bi = b_im.astype(jnp.float32)
    bbr = fr[..., None] * br - fi[..., None] * bi
    bbi = fr[..., None] * bi + fi[..., None] * br
    xr = jnp.einsum('btgh,gph->btgp', uf, bbr)
    xi = jnp.einsum('btgh,gph->btgp', uf, bbi)
    h0r = h0_re.astype(jnp.float32)
    h0i = h0_im.astype(jnp.float32)
    xr = xr.at[:, 0].add(abr * h0r - abi * h0i)
    xi = xi.at[:, 0].add(abr * h0i + abi * h0r)
    ar = jnp.broadcast_to(abr, xr.shape)
    ai = jnp.broadcast_to(abi, xr.shape)
    _, _, hr, hi = lax.associative_scan(_cplx_combine, (ar, ai, xr, xi), axis=1)
    y = (jnp.einsum('btgp,ghp->btgh', hr, c_re.astype(jnp.float32))
         - jnp.einsum('btgp,ghp->btgh', hi, c_im.astype(jnp.float32))
         + d_skip.astype(jnp.float32).reshape(S5_GROUPS, S5_GROUP_CH) * uf)
    return y.reshape(bsz, t, S5_WIDTH), hr[:, -1], hi[:, -1]


def gla_mixer(q, k, v, log_f, s0):
    bsz, t = q.shape[:2]
    c = GLA_CHUNK if t % GLA_CHUNK == 0 else t
    n = t // c

    def blocks(z):
        return z.astype(jnp.float32).reshape(bsz, n, c, GLA_HEADS, -1).transpose(0, 3, 1, 2, 4)

    qb, kb, vb, fb = blocks(q), blocks(k), blocks(v), blocks(log_f)
    b = jnp.cumsum(fb, axis=3)
    b_last = b[:, :, :, -1:, :]
    qs = qb * (GLA_DK ** -0.5) * jnp.exp(b)
    k_intra = kb * jnp.exp(-b)
    k_end = kb * jnp.exp(b_last - b)
    mask = jnp.tril(jnp.ones((c, c), dtype=bool))
    att = jnp.where(mask, jnp.einsum('bhncd,bhnsd->bhncs', qs, k_intra), 0.0)
    o_intra = jnp.einsum('bhncs,bhnse->bhnce', att, vb)
    upd = jnp.einsum('bhnsd,bhnse->bhnde', k_end, vb)
    decay = jnp.exp(b_last[:, :, :, 0, :])

    def step(s, inp):
        dec, u = inp
        return dec[..., None] * s + u, s

    s_final, s_start = lax.scan(step, s0.astype(jnp.float32),
                                (decay.transpose(2, 0, 1, 3), upd.transpose(2, 0, 1, 3, 4)))
    s_start = s_start.transpose(1, 2, 0, 3, 4)
    o = o_intra + jnp.einsum('bhncd,bhnde->bhnce', qs, s_start)
    o = o.transpose(0, 2, 3, 1, 4).reshape(bsz, t, GLA_HEADS, GLA_DV)
    return o, s_final


def token_mixer(h, h0_re, h0_im, s0, w_in, lam_re, lam_im, log_dt, b_re, b_im, c_re, c_im, d_skip,
                glu_w, glu_b, gate_w, gate_b, gla_norm, w_out):
    bsz, t, _ = h.shape
    p = h @ w_in
    u = p[..., OFF_U:OFF_Q]
    q = p[..., OFF_Q:OFF_K]
    k = p[..., OFF_K:OFF_V]
    v = p[..., OFF_V:OFF_G]
    g = p[..., OFF_G:OFF_A]
    a = p[..., OFF_A:IN_WIDTH]
    y_s5, hr, hi = s5_mixer(u, h0_re, h0_im, lam_re, lam_im, log_dt, b_re, b_im, c_re, c_im, d_skip)
    z = jax.nn.gelu(y_s5.astype(h.dtype))
    z = z * jax.nn.sigmoid(z @ glu_w + glu_b)
    log_f = jax.nn.log_sigmoid((a @ gate_w + gate_b).astype(jnp.float32)) / GLA_GATE_NORM
    o, s_new = gla_mixer(q, k, v, log_f, s0)
    o = o * lax.rsqrt(jnp.mean(o * o, axis=-1, keepdims=True) + EPS) * gla_norm.astype(jnp.float32)
    o = o.reshape(bsz, t, GLA_WIDTH).astype(h.dtype) * jax.nn.silu(g)
    out = jnp.concatenate([z, o], axis=-1) @ w_out
    return out, hr, hi, s_new


def decoder(x, s5_re0, s5_im0, gla0, norm_ffn1, ffn1_gate, ffn1_up, ffn1_down, norm_mix, w_in,
            s5_lam_re, s5_lam_im, s5_log_dt, s5_b_re, s5_b_im, s5_c_re, s5_c_im, s5_d, s5_glu_w, s5_glu_b,
            gla_gate_w, gla_gate_b, gla_norm, w_out, norm_ffn2, ffn2_gate, ffn2_up, ffn2_down, norm_final):
    new_re, new_im, new_gla = [], [], []
    for l in range(DEPTH):
        x = x + 0.5 * swiglu(rmsnorm(x, norm_ffn1[l]), ffn1_gate[l], ffn1_up[l], ffn1_down[l])
        mix, hr, hi, s = token_mixer(rmsnorm(x, norm_mix[l]), s5_re0[l], s5_im0[l], gla0[l], w_in[l],
                                     s5_lam_re[l], s5_lam_im[l], s5_log_dt[l], s5_b_re[l], s5_b_im[l],
                                     s5_c_re[l], s5_c_im[l], s5_d[l], s5_glu_w[l], s5_glu_b[l],
                                     gla_gate_w[l], gla_gate_b[l], gla_norm[l], w_out[l])
        x = x + mix
        x = x + 0.5 * swiglu(rmsnorm(x, norm_ffn2[l]), ffn2_gate[l], ffn2_up[l], ffn2_down[l])
        new_re.append(hr)
        new_im.append(hi)
        new_gla.append(s)
    y = rmsnorm(x, norm_final)
    return y, jnp.stack(new_re), jnp.stack(new_im), jnp.stack(new_gla)


def setup_inputs(seed: int = 0) -> dict:
    key = jax.random.key(seed)
    ks = jax.random.split(key, 40)
    f32 = jnp.float32

    def nrm(i, shape, scale):
        return jax.random.normal(ks[i], shape, f32) * scale

    G, P, HS = S5_GROUPS, S5_STATE, S5_GROUP_CH
    lam_im = jnp.broadcast_to(math.pi * jnp.arange(P, dtype=f32), (DEPTH, G, P)) + nrm(5, (DEPTH, G, P), 0.01)
    return {
        'x_prompt': nrm(0, (BATCH, SEQ, D_MODEL), 1.0),
        'x_sample': nrm(1, (DEC_BATCH, DEC_SEQ, D_MODEL), 1.0),
        'state_s5_re': nrm(2, (DEPTH, DEC_BATCH, G, P), 0.5),
        'state_s5_im': nrm(3, (DEPTH, DEC_BATCH, G, P), 0.5),
        'state_gla': nrm(4, (DEPTH, DEC_BATCH, GLA_HEADS, GLA_DK, GLA_DV), 1.0),
        'norm_ffn1': 1.0 + nrm(6, (DEPTH, D_MODEL), 0.01),
        'ffn1_gate': nrm(7, (DEPTH, D_MODEL, D_FF), D_MODEL ** -0.5),
        'ffn1_up': nrm(8, (DEPTH, D_MODEL, D_FF), D_MODEL ** -0.5),
        'ffn1_down': nrm(9, (DEPTH, D_FF, D_MODEL), D_FF ** -0.5),
        'norm_mix': 1.0 + nrm(10, (DEPTH, D_MODEL), 0.01),
        'w_in': nrm(11, (DEPTH, D_MODEL, IN_WIDTH), D_MODEL ** -0.5),
        's5_lam_re': -0.5 + nrm(12, (DEPTH, G, P), 0.01),
        's5_lam_im': lam_im,
        's5_log_dt': jax.random.uniform(ks[13], (DEPTH, G), f32, math.log(DT_MIN), math.log(DT_MAX)),
        's5_b_re': nrm(14, (DEPTH, G, P, HS), (2 * HS) ** -0.5),
        's5_b_im': nrm(15, (DEPTH, G, P, HS), (2 * HS) ** -0.5),
        's5_c_re': nrm(16, (DEPTH, G, HS, P), (2 * P) ** -0.5),
        's5_c_im': nrm(17, (DEPTH, G, HS, P), (2 * P) ** -0.5),
        's5_d': nrm(18, (DEPTH, S5_WIDTH), 1.0),
        's5_glu_w': nrm(19, (DEPTH, S5_WIDTH, S5_WIDTH), S5_WIDTH ** -0.5),
        's5_glu_b': nrm(20, (DEPTH, S5_WIDTH), 0.01),
        'gla_gate_w': nrm(21, (DEPTH, GLA_GATE_RANK, GLA_QK_WIDTH), GLA_GATE_RANK ** -0.5),
        'gla_gate_b': nrm(22, (DEPTH, GLA_QK_WIDTH), 0.1),
        'gla_norm': 1.0 + nrm(23, (DEPTH, GLA_DV), 0.01),
        'w_out': nrm(24, (DEPTH, MIX_WIDTH, D_MODEL), MIX_WIDTH ** -0.5),
        'norm_ffn2': 1.0 + nrm(25, (DEPTH, D_MODEL), 0.01),
        'ffn2_gate': nrm(26, (DEPTH, D_MODEL, D_FF), D_MODEL ** -0.5),
        'ffn2_up': nrm(27, (DEPTH, D_MODEL, D_FF), D_MODEL ** -0.5),
        'ffn2_down': nrm(28, (DEPTH, D_FF, D_MODEL), D_FF ** -0.5),
        'norm_final': 1.0 + nrm(29, (D_MODEL,), 0.01),
    }


def reference(x_prompt, x_sample, state_s5_re, state_s5_im, state_gla, norm_ffn1, ffn1_gate, ffn1_up,
              ffn1_down, norm_mix, w_in, s5_lam_re, s5_lam_im, s5_log_dt, s5_b_re, s5_b_im, s5_c_re,
              s5_c_im, s5_d, s5_glu_w, s5_glu_b, gla_gate_w, gla_gate_b, gla_norm, w_out, norm_ffn2,
              ffn2_gate, ffn2_up, ffn2_down, norm_final):
    weights = (norm_ffn1, ffn1_gate, ffn1_up, ffn1_down, norm_mix, w_in, s5_lam_re, s5_lam_im, s5_log_dt,
               s5_b_re, s5_b_im, s5_c_re, s5_c_im, s5_d, s5_glu_w, s5_glu_b, gla_gate_w, gla_gate_b,
               gla_norm, w_out, norm_ffn2, ffn2_gate, ffn2_up, ffn2_down, norm_final)
    bp = x_prompt.shape[0]
    zero_s5 = jnp.zeros((DEPTH, bp, S5_GROUPS, S5_STATE), jnp.float32)
    zero_gla = jnp.zeros((DEPTH, bp, GLA_HEADS, GLA_DK, GLA_DV), jnp.float32)
    y_prompt, p_re, p_im, p_gla = decoder(x_prompt, zero_s5, zero_s5, zero_gla, *weights)
    y_sample, s_re, s_im, s_gla = decoder(x_sample, state_s5_re, state_s5_im, state_gla, *weights)
    return (y_prompt, y_sample, p_re, p_im, p_gla, s_re, s_im, s_gla)
```

```python
import functools
import math

import jax
import jax.numpy as jnp
import numpy as np
from jax import lax
from jax.experimental import pallas as pl
from jax.experimental.pallas import tpu as pltpu

F32 = jnp.float32
BF16 = jnp.bfloat16

EPS = 1e-6
S5_GROUP_CH = 16
S5_STATE = 64
GLA_HEADS = 4
GLA_DK = 64
GLA_DV = 128
GLA_GATE_RANK = 16
GLA_GATE_NORM = 16.0
GLA_CHUNK = 64

V7X_VMEM_LIMIT_BYTES = 56 * 1024 * 1024
LANES = 128
SUBLANES = 8
ROW_TILE = 512
S5_COL_HALF = 512


def _dot(a, b):
    return jnp.dot(a, b, preferred_element_type=F32)


def _rms(x, g):
    return x * lax.rsqrt(jnp.mean(x * x, axis=-1, keepdims=True) + EPS) * g


def _params(semantics):
    return pltpu.CompilerParams(dimension_semantics=semantics, vmem_limit_bytes=V7X_VMEM_LIMIT_BYTES)


def _ffn_body(*refs, with_mix, final_norm):
    if with_mix:
        (x_ref, zz_ref, o_ref, wout_ref, g_ref, wg_ref, wu_ref, wd_ref, gf_ref,
         out_ref, h_ref, acc_ref, x2_ref) = refs
    else:
        x_ref, g_ref, wg_ref, wu_ref, wd_ref, gf_ref, out_ref, h_ref, acc_ref = refs
        x2_ref = x_ref
    f = pl.program_id(1)
    nf = pl.num_programs(1)

    @pl.when(f == 0)
    def _():
        if with_mix:
            x2_ref[...] = x_ref[...] + _dot(zz_ref[...], wout_ref[0]) + _dot(o_ref[...], wout_ref[1])
        h_ref[...] = _rms(x2_ref[...], g_ref[...]).astype(BF16)

    h = h_ref[...]
    gate = _dot(h, wg_ref[...])
    up = _dot(h, wu_ref[...])
    act = (gate * jax.nn.sigmoid(gate) * up).astype(BF16)
    part = _dot(act, wd_ref[...])

    @pl.when(f == 0)
    def _():
        acc_ref[...] = part

    @pl.when(f > 0)
    def _():
        acc_ref[...] += part

    @pl.when(f == nf - 1)
    def _():
        y = x2_ref[...] + 0.5 * acc_ref[...]
        if final_norm:
            y = _rms(y, gf_ref[...])
        out_ref[...] = y


def _ffn_call(x, g, wg, wu, wd, gf, *, final_norm, mix=None):
    n, d = x.shape
    dff = wg.shape[1]
    tf = dff // 2
    assert n % ROW_TILE == 0 and dff % (2 * LANES) == 0
    tm = ROW_TILE
    row = lambda c: pl.BlockSpec((tm, c), lambda i, f: (i, 0))
    vec = pl.BlockSpec((1, d), lambda i, f: (0, 0))
    w_specs = [vec,
               pl.BlockSpec((d, tf), lambda i, f: (0, f)),
               pl.BlockSpec((d, tf), lambda i, f: (0, f)),
               pl.BlockSpec((tf, d), lambda i, f: (f, 0)),
               vec]
    scratch = [pltpu.VMEM((tm, d), BF16), pltpu.VMEM((tm, d), F32)]
    if mix is None:
        in_specs = [row(d)] + w_specs
        args = (x, g, wg, wu, wd, gf)
    else:
        zz, o, wout = mix
        in_specs = [row(d), row(zz.shape[1]), row(o.shape[1]),
                    pl.BlockSpec(wout.shape, lambda i, f: (0, 0, 0))] + w_specs
        args = (x, zz, o, wout, g, wg, wu, wd, gf)
        scratch.append(pltpu.VMEM((tm, d), F32))
    return pl.pallas_call(
        functools.partial(_ffn_body, with_mix=mix is not None, final_norm=final_norm),
        grid=(n // tm, dff // tf),
        in_specs=in_specs,
        out_specs=row(d),
        out_shape=jax.ShapeDtypeStruct((n, d), F32),
        scratch_shapes=scratch,
        compiler_params=_params(("parallel", "arbitrary")),
        name="ffn_post" if mix is not None else "ffn",
    )(*args)


def _split3(x):
    hi = x.astype(BF16)
    r1 = x - hi.astype(F32)
    mid = r1.astype(BF16)
    lo = (r1 - mid.astype(F32)).astype(BF16)
    return hi, mid, lo


def _s5_scan(xs_ref, st_ref, ar_ref, ai_ref, m, *, nb, seq):
    half_cols = xs_ref.shape[1] // 2
    for c0 in range(0, half_cols, S5_COL_HALF):
        re = slice(c0, c0 + S5_COL_HALF)
        im = slice(half_cols + c0, half_cols + c0 + S5_COL_HALF)
        st_re = slice(2 * half_cols * m + c0, 2 * half_cols * m + c0 + S5_COL_HALF)
        st_im = slice(2 * half_cols * m + half_cols + c0, 2 * half_cols * m + half_cols + c0 + S5_COL_HALF)
        ar = jnp.broadcast_to(ar_ref[m:m + 1, re], (SUBLANES, S5_COL_HALF))
        ai = jnp.broadcast_to(ai_ref[m:m + 1, re], (SUBLANES, S5_COL_HALF))
        for r0 in range(0, nb, SUBLANES):
            rows = slice(r0, r0 + SUBLANES)

            def step(t, carry, r0=r0, re=re, im=im, ar=ar, ai=ai):
                hr, hi = carry
                row = t * nb + r0
                if not isinstance(row, int):
                    row = pl.multiple_of(row, SUBLANES)
                xr = xs_ref[pl.ds(row, SUBLANES), re]
                xi = xs_ref[pl.ds(row, SUBLANES), im]
                nhr = ar * hr - ai * hi + xr
                nhi = ar * hi + ai * hr + xi
                xs_ref[pl.ds(row, SUBLANES), re] = nhr
                xs_ref[pl.ds(row, SUBLANES), im] = nhi
                return nhr, nhi

            carry = (st_ref[rows, st_re], st_ref[rows, st_im])
            if seq <= 8:
                for t in range(seq):
                    carry = step(t, carry)
            else:
                carry = lax.fori_loop(0, seq, step, carry, unroll=4)
            st_ref[rows, st_re] = carry[0]
            st_ref[rows, st_im] = carry[1]


def _mixer_pre_body(x_ref, s0_ref, g_ref, win_ref, wa_ref, perm_ref, permt_ref, bbig_ref, cbig_ref,
                    ar_ref, ai_ref, dskip_ref, gluw_ref, glub_ref, gatew_ref, gateb_ref, mcs_ref,
                    zz_ref, qs_ref, kin_ref, kend_ref, v_ref, gs_ref, ef_ref, sfin_ref,
                    st_ref, xs_ref, *, nb, seq):
    rows = nb * seq
    d = x_ref.shape[-1]
    s5w = dskip_ref.shape[-1]
    qkw = gateb_ref.shape[-1]
    vw = gs_ref.shape[-1]

    @pl.when(pl.program_id(0) == 0)
    def _():
        st_ref[...] = s0_ref[...]

    x = x_ref[...].reshape(rows, d)
    h = _rms(x, g_ref[...]).astype(BF16)
    p = _dot(h, win_ref[...])
    a = _dot(h, wa_ref[...])
    u = p[:, :s5w]
    q = p[:, s5w:s5w + qkw]
    k = p[:, s5w + qkw:s5w + 2 * qkw]
    v = p[:, s5w + 2 * qkw:s5w + 2 * qkw + vw]
    g = p[:, s5w + 2 * qkw + vw:]

    u_hi = u.astype(BF16)
    u_lo = (u - u_hi.astype(F32)).astype(BF16)
    perm = perm_ref[...]
    up_hi = _dot(perm, u_hi)
    up = up_hi + _dot(perm, u_lo)
    up_hi = up_hi.astype(BF16)
    n_blk = bbig_ref.shape[0]
    ch_blk = bbig_ref.shape[1]
    ys = []
    for m in range(n_blk):
        xs_ref[...] = _dot(up_hi[:, ch_blk * m:ch_blk * (m + 1)], bbig_ref[m])
        _s5_scan(xs_ref, st_ref, ar_ref, ai_ref, m, nb=nb, seq=seq)
        ys.append(_dot(xs_ref[...].astype(BF16), cbig_ref[m]))
    y = jnp.concatenate(ys, axis=1) + dskip_ref[...] * up
    z = jax.nn.gelu(y)
    z = z * jax.nn.sigmoid(_dot(z.astype(BF16), gluw_ref[...]) + glub_ref[...])
    zz = _dot(permt_ref[...], z.astype(BF16)).astype(BF16)
    zz_ref[...] = zz.reshape(zz_ref.shape)
    sfin_ref[...] = st_ref[...]

    gate = _dot(a.astype(BF16), gatew_ref[...]) + gateb_ref[...]
    log_f = (jnp.minimum(gate, 0.0) - jnp.log(1.0 + jnp.exp(-jnp.abs(gate)))) / GLA_GATE_NORM
    pieces = _split3(log_f)
    mcs = mcs_ref[...]
    grp = mcs.shape[1]
    bcum, blast = [], []
    for r0 in range(0, rows, grp):
        cs = [_dot(mcs, piece[r0:r0 + grp]) for piece in pieces]
        tot = (cs[0] + cs[1]) + cs[2]
        bcum.append(tot[:grp])
        blast.append(tot[grp:])
    bcum = jnp.concatenate(bcum, axis=0)
    blast = jnp.concatenate(blast, axis=0)
    qs = q * (GLA_DK ** -0.5) * jnp.exp(bcum)
    kin = k * jnp.exp(-bcum)
    kend = k * jnp.exp(blast - bcum)
    qs_ref[...] = qs.astype(BF16).reshape(qs_ref.shape)
    kin_ref[...] = kin.astype(BF16).reshape(kin_ref.shape)
    kend_ref[...] = kend.astype(BF16).reshape(kend_ref.shape)
    v_ref[...] = v.astype(BF16).reshape(v_ref.shape)
    gs_ref[...] = (g * jax.nn.sigmoid(g)).reshape(gs_ref.shape)
    ef_ref[...] = jnp.exp(blast).reshape(ef_ref.shape)


def _mixer_pre_call(x3, s0, w, *, nb, seq, blk):
    bv, tv, d = x3.shape
    bb, tt = blk
    rows = nb * seq
    assert bb * tt == rows and bv == bb and tv % tt == 0
    s5w = w["dskip"].shape[1]
    qkw = w["gate_b"].shape[1]
    vw = (w["w_in"].shape[1] - s5w - 2 * qkw) // 2
    tok = lambda c: pl.BlockSpec((bb, tt, c), lambda i: (0, i, 0))
    full = lambda arr: pl.BlockSpec(arr.shape, lambda i, nd=arr.ndim: (0,) * nd)
    consts = (w["norm_mix"], w["w_in"], w["w_a"], w["perm"], w["perm_t"], w["bbig"], w["cbig"], w["abar_re"],
              w["abar_im"], w["dskip"], w["glu_w"], w["glu_b"], w["gate_w"], w["gate_b"], w["mcs"])
    tok_shape = lambda c, dt: jax.ShapeDtypeStruct((bv, tv, c), dt)
    outs = pl.pallas_call(
        functools.partial(_mixer_pre_body, nb=nb, seq=seq),
        grid=(tv // tt,),
        in_specs=[tok(d), full(s0)] + [full(c) for c in consts],
        out_specs=[tok(s5w), tok(qkw), tok(qkw), tok(qkw), tok(vw), tok(vw), tok(qkw), full(s0)],
        out_shape=[tok_shape(s5w, BF16), tok_shape(qkw, BF16), tok_shape(qkw, BF16), tok_shape(qkw, BF16),
                   tok_shape(vw, BF16), tok_shape(vw, F32), tok_shape(qkw, F32),
                   jax.ShapeDtypeStruct(s0.shape, F32)],
        scratch_shapes=[pltpu.VMEM(s0.shape, F32), pltpu.VMEM((rows, w["bbig"].shape[2]), F32)],
        compiler_params=_params(("arbitrary",)),
        name="mixer_pre",
    )(x3, s0, *consts)
    return outs


def _gla_body(qs_ref, kin_ref, kend_ref, v_ref, gs_ref, ef_ref, s0_ref, m1_ref, m2_ref, m2t_ref, m3_ref,
              rt_ref, gn_ref, o_ref, sout_ref, s_ref, *, rg, seq):
    nbg = rg // seq
    c = pl.program_id(1)

    @pl.when(c == 0)
    def _():
        for hd in range(GLA_HEADS):
            for b in range(nbg):
                s_ref[hd, GLA_DK * b:GLA_DK * (b + 1), :] = s0_ref[b, hd]

    qkw = GLA_HEADS * GLA_DK
    qs = qs_ref[...].reshape(rg, qkw).astype(F32)
    kin = kin_ref[...].reshape(rg, qkw)
    kend_t = jnp.transpose(kend_ref[...].reshape(rg, qkw).astype(F32))
    ef_t = jnp.transpose(ef_ref[...].reshape(rg, qkw))
    v = v_ref[...].reshape(rg, GLA_HEADS * GLA_DV)
    gs = gs_ref[...].reshape(rg, GLA_HEADS * GLA_DV)
    lane_head = lax.broadcasted_iota(jnp.int32, (rg, qkw), 1) // GLA_DK
    m1 = m1_ref[...] > 0
    m2 = m2_ref[...] > 0
    m2t = m2t_ref[...] > 0
    m3 = m3_ref[...] > 0
    rt = rt_ref[...]
    outs = []
    for hd in range(GLA_HEADS):
        q_h = jnp.where(lane_head == hd, qs, 0.0).astype(BF16)
        att = lax.dot_general(q_h, kin, (((1,), (1,)), ((), ())), preferred_element_type=F32)
        att = jnp.where(m1, att, 0.0).astype(BF16)
        v_h = v[:, GLA_DV * hd:GLA_DV * (hd + 1)]
        q_exp = jnp.where(m2, _dot(q_h, rt), 0.0).astype(BF16)
        s_h = s_ref[hd]
        o = _dot(att, v_h) + _dot(q_exp, s_h.astype(BF16))
        k_t = jnp.concatenate([kend_t[GLA_DK * hd:GLA_DK * (hd + 1)]] * nbg, axis=0)
        k_exp_t = jnp.where(m2t, k_t, 0.0).astype(BF16)
        upd = _dot(k_exp_t, v_h)
        e_t = jnp.concatenate([ef_t[GLA_DK * hd:GLA_DK * (hd + 1)]] * nbg, axis=0)
        decay = jnp.sum(jnp.where(m3, e_t, 0.0), axis=1, keepdims=True)
        s_ref[hd] = decay * s_h + upd
        o = o * lax.rsqrt(jnp.mean(o * o, axis=-1, keepdims=True) + EPS) * gn_ref[...]
        outs.append((o * gs[:, GLA_DV * hd:GLA_DV * (hd + 1)]).astype(BF16))
    o_ref[...] = jnp.concatenate(outs, axis=1).reshape(o_ref.shape)

    @pl.when(c == pl.num_programs(1) - 1)
    def _():
        for hd in range(GLA_HEADS):
            for b in range(nbg):
                sout_ref[b, hd] = s_ref[hd, GLA_DK * b:GLA_DK * (b + 1), :]


def _gla_masks(rg, seq):
    nbg = rg // seq
    ns = nbg * GLA_DK
    tok = np.arange(rg)
    tb, tt = tok // seq, tok % seq
    srow = np.arange(ns)
    sb, sd = srow // GLA_DK, srow % GLA_DK
    m1 = (tb[:, None] == tb[None, :]) & (tt[None, :] <= tt[:, None])
    m2 = tb[:, None] == sb[None, :]
    m3 = (sb[:, None] * seq) == tok[None, :]
    rt = (np.arange(GLA_HEADS * GLA_DK)[:, None] % GLA_DK) == sd[None, :]
    as_f32 = lambda a: jnp.asarray(a.astype(np.float32))
    return as_f32(m1), as_f32(m2), as_f32(m2.T), as_f32(m3), jnp.asarray(rt.astype(np.float32), dtype=BF16)


def _gla_call(qs, kin, kend, v, gs, ef, s0, gn, *, rg, seq, blk, grid, tok_index, state_index):
    bb, tt = blk
    assert bb * tt == rg and rg % seq == 0
    nbg = rg // seq
    ns = nbg * GLA_DK
    masks = _gla_masks(rg, seq)
    tok = lambda arr: pl.BlockSpec((bb, tt, arr.shape[2]), tok_index)
    full = lambda arr: pl.BlockSpec(arr.shape, lambda g, c, nd=arr.ndim: (0,) * nd)
    st = pl.BlockSpec((nbg,) + s0.shape[1:], state_index)
    return pl.pallas_call(
        functools.partial(_gla_body, rg=rg, seq=seq),
        grid=grid,
        in_specs=[tok(qs), tok(kin), tok(kend), tok(v), tok(gs), tok(ef), st]
                 + [full(m) for m in masks] + [full(gn)],
        out_specs=[tok(v), st],
        out_shape=[jax.ShapeDtypeStruct(v.shape, BF16), jax.ShapeDtypeStruct(s0.shape, F32)],
        scratch_shapes=[pltpu.VMEM((GLA_HEADS, ns, GLA_DV), F32)],
        compiler_params=_params(("parallel", "arbitrary")),
        name="gla",
    )(qs, kin, kend, v, gs, ef, s0, *masks, gn)


def _s5_discretise(lam_re, lam_im, log_dt, b_re, b_im):
    dt = jnp.exp(log_dt.astype(F32))[:, None]
    lr = lam_re.astype(F32)
    li = lam_im.astype(F32)
    mag = jnp.exp(lr * dt)
    abr = mag * jnp.cos(li * dt)
    abi = mag * jnp.sin(li * dt)
    den = lr * lr + li * li
    fr = ((abr - 1.0) * lr + abi * li) / den
    fi = (abi * lr - (abr - 1.0) * li) / den
    br = b_re.astype(F32)
    bi = b_im.astype(F32)
    bbr = fr[..., None] * br - fi[..., None] * bi
    bbi = fr[..., None] * bi + fi[..., None] * br
    return abr, abi, bbr, bbi


def _s5_maps(abr, abi, bbr, bbi, c_re, c_im, n_blk):
    groups, states, ch = bbr.shape
    gpb = groups // n_blk
    eye = jnp.eye(gpb, dtype=F32)

    def b_map(bb):
        return jnp.einsum("gph,gk->ghkp", bb, eye).reshape(gpb * ch, gpb * states)

    def c_map(cc):
        return jnp.einsum("ghp,gk->gpkh", cc, eye).reshape(gpb * states, gpb * ch)

    bbig, cbig, ar, ai = [], [], [], []
    for m in range(n_blk):
        gsl = slice(gpb * m, gpb * (m + 1))
        bbig.append(jnp.concatenate([b_map(bbr[gsl]), b_map(bbi[gsl])], axis=1))
        cbig.append(jnp.concatenate([c_map(c_re[gsl].astype(F32)), -c_map(c_im[gsl].astype(F32))], axis=0))
        ar.append(abr[gsl].reshape(-1))
        ai.append(abi[gsl].reshape(-1))
    return (jnp.stack(bbig).astype(BF16), jnp.stack(cbig).astype(BF16), jnp.stack(ar), jnp.stack(ai))


def _s5_state_pack(re, im, n_blk):
    b = re.shape[0]
    re = re.astype(F32).reshape(b, n_blk, -1)
    im = im.astype(F32).reshape(b, n_blk, -1)
    return jnp.stack([re, im], axis=2).reshape(b, -1)


def _s5_state_unpack(st, n_blk, groups, states):
    b = st.shape[0]
    st = st.reshape(b, n_blk, 2, -1)
    return st[:, :, 0].reshape(b, groups, states), st[:, :, 1].reshape(b, groups, states)


def _perm_matrix(nb, seq):
    rows = nb * seq
    dst = np.arange(rows)
    src = (dst % nb) * seq + dst // nb
    p = np.zeros((rows, rows), np.float32)
    p[dst, src] = 1.0
    return jnp.asarray(p, dtype=BF16), jnp.asarray(p.T, dtype=BF16)


def _cumsum_masks(grp, seq):
    tok = np.arange(grp)
    tb, tt = tok // seq, tok % seq
    same = tb[:, None] == tb[None, :]
    m = np.concatenate([same & (tt[None, :] <= tt[:, None]), same], axis=0)
    return jnp.asarray(m.astype(np.float32), dtype=BF16)


def _layer_weights(l, norm_ffn1, ffn1_gate, ffn1_up, ffn1_down, norm_mix, w_in, s5_lam_re, s5_lam_im, s5_log_dt,
                   s5_b_re, s5_b_im, s5_c_re, s5_c_im, s5_d, s5_glu_w, s5_glu_b, gla_gate_w, gla_gate_b, gla_norm,
                   w_out, norm_ffn2, ffn2_gate, ffn2_up, ffn2_down):
    d = w_in.shape[1]
    s5w = s5_d.shape[1]
    qkw = gla_gate_b.shape[1]
    main = w_in.shape[2] - GLA_GATE_RANK
    n_blk = 2
    abr, abi, bbr, bbi = _s5_discretise(s5_lam_re[l], s5_lam_im[l], s5_log_dt[l], s5_b_re[l], s5_b_im[l])
    bbig, cbig, ar, ai = _s5_maps(abr, abi, bbr, bbi, s5_c_re[l], s5_c_im[l], n_blk)
    w_a = jnp.zeros((d, LANES), F32).at[:, :GLA_GATE_RANK].set(w_in[l][:, main:])
    gate_w = jnp.zeros((LANES, qkw), F32).at[:GLA_GATE_RANK].set(gla_gate_w[l])
    row = lambda a: a.astype(F32).reshape(1, -1)
    return dict(
        norm_ffn1=row(norm_ffn1[l]), ffn1=(ffn1_gate[l].astype(BF16), ffn1_up[l].astype(BF16), ffn1_down[l].astype(BF16)),
        norm_ffn2=row(norm_ffn2[l]), ffn2=(ffn2_gate[l].astype(BF16), ffn2_up[l].astype(BF16), ffn2_down[l].astype(BF16)),
        norm_mix=row(norm_mix[l]), w_in=w_in[l][:, :main].astype(BF16), w_a=w_a.astype(BF16),
        bbig=bbig, cbig=cbig, abar_re=ar, abar_im=ai, dskip=row(s5_d[l]),
        glu_w=s5_glu_w[l].astype(BF16), glu_b=row(s5_glu_b[l]),
        gate_w=gate_w.astype(BF16), gate_b=row(gla_gate_b[l]), gla_norm=row(gla_norm[l]),
        w_out=w_out[l].astype(BF16).reshape(2, s5w, -1), n_blk=n_blk)


def _decoder(x, s5_re0, s5_im0, gla0, layers, norm_final, *, prompt):
    b, t, d = x.shape
    if prompt:
        seq = GLA_CHUNK if t % GLA_CHUNK == 0 else t
        nb = b
        view = (b, t)
        pre_blk = (b, seq)
        rg = 256
        gla_blk = (rg // seq, seq)
        gla_grid = (b // gla_blk[0], t // seq)
        tok_index = lambda g, c: (g, c, 0)
    else:
        seq = t
        nb = b
        view = (1, b * t)
        pre_blk = (1, b * t)
        rg = 128
        gla_blk = (1, rg)
        gla_grid = (b * t // rg, 1)
        tok_index = lambda g, c: (0, g, 0)
    state_index = lambda g, c: (g, 0, 0, 0)
    assert nb * seq == ROW_TILE and nb % SUBLANES == 0
    n = b * t
    perm, perm_t = _perm_matrix(nb, seq)
    mcs = _cumsum_masks(256, seq)
    gf = norm_final.astype(F32).reshape(1, -1)
    xf = x.astype(F32).reshape(n, d)
    new_re, new_im, new_gla = [], [], []
    for l, w in enumerate(layers):
        w = dict(w, perm=perm, perm_t=perm_t, mcs=mcs)
        groups, states = s5_re0.shape[2], s5_re0.shape[3]
        x1 = _ffn_call(xf, w["norm_ffn1"], *w["ffn1"], gf, final_norm=False)
        s0 = _s5_state_pack(s5_re0[l], s5_im0[l], w["n_blk"])
        zz, qs, kin, kend, v, gs, ef, sfin = _mixer_pre_call(
            x1.reshape(view + (d,)), s0, w, nb=nb, seq=seq, blk=pre_blk)
        o, s_new = _gla_call(qs, kin, kend, v, gs, ef, gla0[l].astype(F32), w["gla_norm"], rg=rg, seq=seq,
                             blk=gla_blk, grid=gla_grid, tok_index=tok_index, state_index=state_index)
        last = l == len(layers) - 1
        xf = _ffn_call(x1, w["norm_ffn2"], *w["ffn2"], gf, final_norm=last,
                       mix=(zz.reshape(n, -1), o.reshape(n, -1), w["w_out"]))
        hr, hi = _s5_state_unpack(sfin, w["n_blk"], groups, states)
        new_re.append(hr)
        new_im.append(hi)
        new_gla.append(s_new)
    return xf.reshape(b, t, d), jnp.stack(new_re), jnp.stack(new_im), jnp.stack(new_gla)


def kernel(x_prompt, x_sample, state_s5_re, state_s5_im, state_gla, norm_ffn1, ffn1_gate, ffn1_up, ffn1_down, norm_mix, w_in, s5_lam_re, s5_lam_im, s5_log_dt, s5_b_re, s5_b_im, s5_c_re, s5_c_im, s5_d, s5_glu_w, s5_glu_b, gla_gate_w, gla_gate_b, gla_norm, w_out, norm_ffn2, ffn2_gate, ffn2_up, ffn2_down, norm_final):
    depth = norm_ffn1.shape[0]
    layers = [_layer_weights(l, norm_ffn1, ffn1_gate, ffn1_up, ffn1_down, norm_mix, w_in, s5_lam_re, s5_lam_im,
                             s5_log_dt, s5_b_re, s5_b_im, s5_c_re, s5_c_im, s5_d, s5_glu_w, s5_glu_b, gla_gate_w,
                             gla_gate_b, gla_norm, w_out, norm_ffn2, ffn2_gate, ffn2_up, ffn2_down)
              for l in range(depth)]
    bp = x_prompt.shape[0]
    zero_s5 = jnp.zeros((depth, bp) + state_s5_re.shape[2:], F32)
    zero_gla = jnp.zeros((depth, bp) + state_gla.shape[2:], F32)
    y_p, p_re, p_im, p_gla = _decoder(x_prompt, zero_s5, zero_s5, zero_gla, layers, norm_final, prompt=True)
    y_s, s_re, s_im, s_gla = _decoder(x_sample, state_s5_re, state_s5_im, state_gla, layers, norm_final, prompt=False)
    return (y_p, y_s, p_re, p_im, p_gla, s_re, s_im, s_gla)
```

```python
import functools
import math

import jax
import jax.numpy as jnp
import numpy as np
from jax import lax
from jax.experimental import pallas as pl
from jax.experimental.pallas import tpu as pltpu

F32 = jnp.float32
BF16 = jnp.bfloat16

EPS = 1e-6
S5_GROUP_CH = 16
S5_STATE = 64
GLA_HEADS = 4
GLA_DK = 64
GLA_DV = 128
GLA_GATE_RANK = 16
GLA_GATE_NORM = 16.0
GLA_CHUNK = 64

V7X_VMEM_LIMIT_BYTES = 56 * 1024 * 1024
LANES = 128
SUBLANES = 8
MXU_TILE = 256
ROW_TILE = 512
S5_COL_HALF = 512


def _dot(a, b):
    return jnp.dot(a, b, preferred_element_type=F32)


def _rms(x, g):
    return x * lax.rsqrt(jnp.mean(x * x, axis=-1, keepdims=True) + EPS) * g


def _params(semantics):
    return pltpu.CompilerParams(dimension_semantics=semantics, vmem_limit_bytes=V7X_VMEM_LIMIT_BYTES)


def _ffn_body(*refs, with_mix, final_norm):
    if with_mix:
        x_ref, zz_ref, o_ref, wout_ref, g_ref, wg_ref, wu_ref, wd_ref, gf_ref, out_ref = refs
        x = x_ref[...] + _dot(zz_ref[...], wout_ref[0]) + _dot(o_ref[...], wout_ref[1])
    else:
        x_ref, g_ref, wg_ref, wu_ref, wd_ref, gf_ref, out_ref = refs
        x = x_ref[...]
    h = _rms(x, g_ref[...]).astype(BF16)
    acc = None
    for c0 in range(0, wg_ref.shape[1], MXU_TILE):
        cols = slice(c0, c0 + MXU_TILE)
        gate = _dot(h, wg_ref[:, cols])
        up = _dot(h, wu_ref[:, cols])
        act = (gate * jax.nn.sigmoid(gate) * up).astype(BF16)
        part = _dot(act, wd_ref[cols, :])
        acc = part if acc is None else acc + part
    y = x + 0.5 * acc
    if final_norm:
        y = _rms(y, gf_ref[...])
    out_ref[...] = y


def _ffn_call(x, g, wg, wu, wd, gf, *, final_norm, mix=None):
    n, d = x.shape
    dff = wg.shape[1]
    assert n % ROW_TILE == 0 and dff % MXU_TILE == 0
    tm = ROW_TILE
    row = lambda c: pl.BlockSpec((tm, c), lambda i: (i, 0))
    res = lambda arr: pl.BlockSpec(arr.shape, lambda i, nd=arr.ndim: (0,) * nd, pipeline_mode=pl.Buffered(1))
    w_specs = [res(g), res(wg), res(wu), res(wd), res(gf)]
    if mix is None:
        in_specs = [row(d)] + w_specs
        args = (x, g, wg, wu, wd, gf)
    else:
        zz, o, wout = mix
        in_specs = [row(d), row(zz.shape[1]), row(o.shape[1]), res(wout)] + w_specs
        args = (x, zz, o, wout, g, wg, wu, wd, gf)
    return pl.pallas_call(
        functools.partial(_ffn_body, with_mix=mix is not None, final_norm=final_norm),
        grid=(n // tm,),
        in_specs=in_specs,
        out_specs=row(d),
        out_shape=jax.ShapeDtypeStruct((n, d), F32),
        compiler_params=_params(("parallel",)),
        name="ffn_post" if mix is not None else "ffn",
    )(*args)


def _split3(x):
    hi = x.astype(BF16)
    r1 = x - hi.astype(F32)
    mid = r1.astype(BF16)
    lo = (r1 - mid.astype(F32)).astype(BF16)
    return hi, mid, lo


def _s5_scan(xs_ref, st_ref, ar_ref, ai_ref, m, *, nb, seq):
    half_cols = xs_ref.shape[1] // 2
    for c0 in range(0, half_cols, S5_COL_HALF):
        re = slice(c0, c0 + S5_COL_HALF)
        im = slice(half_cols + c0, half_cols + c0 + S5_COL_HALF)
        st_re = slice(2 * half_cols * m + c0, 2 * half_cols * m + c0 + S5_COL_HALF)
        st_im = slice(2 * half_cols * m + half_cols + c0, 2 * half_cols * m + half_cols + c0 + S5_COL_HALF)
        ar = jnp.broadcast_to(ar_ref[m:m + 1, re], (SUBLANES, S5_COL_HALF))
        ai = jnp.broadcast_to(ai_ref[m:m + 1, re], (SUBLANES, S5_COL_HALF))
        for r0 in range(0, nb, SUBLANES):
            rows = slice(r0, r0 + SUBLANES)

            def step(t, carry, r0=r0, re=re, im=im, ar=ar, ai=ai):
                hr, hi = carry
                row = t * nb + r0
                if not isinstance(row, int):
                    row = pl.multiple_of(row, SUBLANES)
                xr = xs_ref[pl.ds(row, SUBLANES), re]
                xi = xs_ref[pl.ds(row, SUBLANES), im]
                nhr = ar * hr - ai * hi + xr
                nhi = ar * hi + ai * hr + xi
                xs_ref[pl.ds(row, SUBLANES), re] = nhr
                xs_ref[pl.ds(row, SUBLANES), im] = nhi
                return nhr, nhi

            carry = (st_ref[rows, st_re], st_ref[rows, st_im])
            for t in range(seq):
                carry = step(t, carry)
            st_ref[rows, st_re] = carry[0]
            st_ref[rows, st_im] = carry[1]


def _mixer_pre_body(x_ref, s0_ref, g_ref, win_ref, wa_ref, perm_ref, permt_ref, bbig_ref, cbig_ref,
                    ar_ref, ai_ref, dskip_ref, gluw_ref, glub_ref, gatew_ref, gateb_ref, mcs_ref,
                    zz_ref, qs_ref, kin_ref, kend_ref, v_ref, gs_ref, ef_ref, sfin_ref,
                    st_ref, xs_ref, *, nb, seq):
    rows = nb * seq
    d = x_ref.shape[-1]
    s5w = dskip_ref.shape[-1]
    qkw = gateb_ref.shape[-1]
    vw = gs_ref.shape[-1]

    @pl.when(pl.program_id(0) == 0)
    def _():
        st_ref[...] = s0_ref[...]

    x = x_ref[...].reshape(rows, d)
    h = _rms(x, g_ref[...]).astype(BF16)
    p = _dot(h, win_ref[...])
    a = _dot(h, wa_ref[...])
    u = p[:, :s5w]
    q = p[:, s5w:s5w + qkw]
    k = p[:, s5w + qkw:s5w + 2 * qkw]
    v = p[:, s5w + 2 * qkw:s5w + 2 * qkw + vw]
    g = p[:, s5w + 2 * qkw + vw:]

    u_hi = u.astype(BF16)
    u_lo = (u - u_hi.astype(F32)).astype(BF16)
    perm = perm_ref[...]
    up_hi = _dot(perm, u_hi)
    up = up_hi + _dot(perm, u_lo)
    up_hi = up_hi.astype(BF16)
    n_blk = bbig_ref.shape[0]
    ch_blk = bbig_ref.shape[1]
    ys = []
    for m in range(n_blk):
        xs_ref[...] = _dot(up_hi[:, ch_blk * m:ch_blk * (m + 1)], bbig_ref[m])
        _s5_scan(xs_ref, st_ref, ar_ref, ai_ref, m, nb=nb, seq=seq)
        ys.append(_dot(xs_ref[...].astype(BF16), cbig_ref[m]))
    y = jnp.concatenate(ys, axis=1) + dskip_ref[...] * up
    z = jax.nn.gelu(y)
    z = z * jax.nn.sigmoid(_dot(z.astype(BF16), gluw_ref[...]) + glub_ref[...])
    zz = _dot(permt_ref[...], z.astype(BF16)).astype(BF16)
    zz_ref[...] = zz.reshape(zz_ref.shape)
    sfin_ref[...] = st_ref[...]

    gate = _dot(a.astype(BF16), gatew_ref[...]) + gateb_ref[...]
    log_f = (jnp.minimum(gate, 0.0) - jnp.log(1.0 + jnp.exp(-jnp.abs(gate)))) / GLA_GATE_NORM
    pieces = _split3(log_f)
    mcs = mcs_ref[...]
    grp = mcs.shape[1]
    bcum, blast = [], []
    for r0 in range(0, rows, grp):
        cs = [_dot(mcs, piece[r0:r0 + grp]) for piece in pieces]
        tot = (cs[0] + cs[1]) + cs[2]
        bcum.append(tot[:grp])
        blast.append(tot[grp:])
    bcum = jnp.concatenate(bcum, axis=0)
    blast = jnp.concatenate(blast, axis=0)
    qs = q * (GLA_DK ** -0.5) * jnp.exp(bcum)
    kin = k * jnp.exp(-bcum)
    kend = k * jnp.exp(blast - bcum)
    qs_ref[...] = qs.astype(BF16).reshape(qs_ref.shape)
    kin_ref[...] = kin.astype(BF16).reshape(kin_ref.shape)
    kend_ref[...] = kend.astype(BF16).reshape(kend_ref.shape)
    v_ref[...] = v.astype(BF16).reshape(v_ref.shape)
    gs_ref[...] = (g * jax.nn.sigmoid(g)).reshape(gs_ref.shape)
    ef_ref[...] = jnp.exp(blast).reshape(ef_ref.shape)


def _mixer_pre_call(x3, s0, w, *, nb, seq, blk):
    bv, tv, d = x3.shape
    bb, tt = blk
    rows = nb * seq
    assert bb * tt == rows and bv == bb and tv % tt == 0
    s5w = w["dskip"].shape[1]
    qkw = w["gate_b"].shape[1]
    vw = (w["w_in"].shape[1] - s5w - 2 * qkw) // 2
    tok = lambda c: pl.BlockSpec((bb, tt, c), lambda i: (0, i, 0))
    full = lambda arr: pl.BlockSpec(arr.shape, lambda i, nd=arr.ndim: (0,) * nd)
    consts = (w["norm_mix"], w["w_in"], w["w_a"], w["perm"], w["perm_t"], w["bbig"], w["cbig"], w["abar_re"],
              w["abar_im"], w["dskip"], w["glu_w"], w["glu_b"], w["gate_w"], w["gate_b"], w["mcs"])
    tok_shape = lambda c, dt: jax.ShapeDtypeStruct((bv, tv, c), dt)
    outs = pl.pallas_call(
        functools.partial(_mixer_pre_body, nb=nb, seq=seq),
        grid=(tv // tt,),
        in_specs=[tok(d), full(s0)] + [full(c) for c in consts],
        out_specs=[tok(s5w), tok(qkw), tok(qkw), tok(qkw), tok(vw), tok(vw), tok(qkw), full(s0)],
        out_shape=[tok_shape(s5w, BF16), tok_shape(qkw, BF16), tok_shape(qkw, BF16), tok_shape(qkw, BF16),
                   tok_shape(vw, BF16), tok_shape(vw, F32), tok_shape(qkw, F32),
                   jax.ShapeDtypeStruct(s0.shape, F32)],
        scratch_shapes=[pltpu.VMEM(s0.shape, F32), pltpu.VMEM((rows, w["bbig"].shape[2]), F32)],
        compiler_params=_params(("arbitrary",)),
        name="mixer_pre",
    )(x3, s0, *consts)
    return outs


def _gla_body(qs_ref, kin_ref, kend_ref, v_ref, gs_ref, ef_ref, s0_ref, m1_ref, m2_ref, m2t_ref, m3_ref,
              rt_ref, gn_ref, o_ref, sout_ref, s_ref, *, rg, seq):
    nbg = rg // seq
    c = pl.program_id(1)

    @pl.when(c == 0)
    def _():
        for hd in range(GLA_HEADS):
            for b in range(nbg):
                s_ref[hd, GLA_DK * b:GLA_DK * (b + 1), :] = s0_ref[b, hd]

    qkw = GLA_HEADS * GLA_DK
    qs = qs_ref[...].reshape(rg, qkw).astype(F32)
    kin = kin_ref[...].reshape(rg, qkw)
    kend_t = jnp.transpose(kend_ref[...].reshape(rg, qkw).astype(F32))
    ef_t = jnp.transpose(ef_ref[...].reshape(rg, qkw))
    v = v_ref[...].reshape(rg, GLA_HEADS * GLA_DV)
    gs = gs_ref[...].reshape(rg, GLA_HEADS * GLA_DV)
    lane_head = lax.broadcasted_iota(jnp.int32, (rg, qkw), 1) // GLA_DK
    m1 = m1_ref[...] > 0
    m2 = m2_ref[...] > 0
    m2t = m2t_ref[...] > 0
    m3 = m3_ref[...] > 0
    rt = rt_ref[...]
    outs = []
    for hd in range(GLA_HEADS):
        q_h = jnp.where(lane_head == hd, qs, 0.0).astype(BF16)
        att = lax.dot_general(q_h, kin, (((1,), (1,)), ((), ())), preferred_element_type=F32)
        att = jnp.where(m1, att, 0.0).astype(BF16)
        v_h = v[:, GLA_DV * hd:GLA_DV * (hd + 1)]
        q_exp = jnp.where(m2, _dot(q_h, rt), 0.0).astype(BF16)
        s_h = s_ref[hd]
        o = _dot(att, v_h) + _dot(q_exp, s_h.astype(BF16))
        k_t = jnp.concatenate([kend_t[GLA_DK * hd:GLA_DK * (hd + 1)]] * nbg, axis=0)
        k_exp_t = jnp.where(m2t, k_t, 0.0).astype(BF16)
        upd = _dot(k_exp_t, v_h)
        e_t = jnp.concatenate([ef_t[GLA_DK * hd:GLA_DK * (hd + 1)]] * nbg, axis=0)
        decay = jnp.sum(jnp.where(m3, e_t, 0.0), axis=1, keepdims=True)
        s_ref[hd] = decay * s_h + upd
        o = o * lax.rsqrt(jnp.mean(o * o, axis=-1, keepdims=True) + EPS) * gn_ref[...]
        outs.append((o * gs[:, GLA_DV * hd:GLA_DV * (hd + 1)]).astype(BF16))
    o_ref[...] = jnp.concatenate(outs, axis=1).reshape(o_ref.shape)

    @pl.when(c == pl.num_programs(1) - 1)
    def _():
        for hd in range(GLA_HEADS):
            for b in range(nbg):
                sout_ref[b, hd] = s_ref[hd, GLA_DK * b:GLA_DK * (b + 1), :]


def _gla_masks(rg, seq):
    nbg = rg // seq
    ns = nbg * GLA_DK
    tok = np.arange(rg)
    tb, tt = tok // seq, tok % seq
    srow = np.arange(ns)
    sb, sd = srow // GLA_DK, srow % GLA_DK
    m1 = (tb[:, None] == tb[None, :]) & (tt[None, :] <= tt[:, None])
    m2 = tb[:, None] == sb[None, :]
    m3 = (sb[:, None] * seq) == tok[None, :]
    rt = (np.arange(GLA_HEADS * GLA_DK)[:, None] % GLA_DK) == sd[None, :]
    as_f32 = lambda a: jnp.asarray(a.astype(np.float32))
    return as_f32(m1), as_f32(m2), as_f32(m2.T), as_f32(m3), jnp.asarray(rt.astype(np.float32), dtype=BF16)


def _gla_call(qs, kin, kend, v, gs, ef, s0, gn, *, rg, seq, blk, grid, tok_index, state_index):
    bb, tt = blk
    assert bb * tt == rg and rg % seq == 0
    nbg = rg // seq
    ns = nbg * GLA_DK
    masks = _gla_masks(rg, seq)
    tok = lambda arr: pl.BlockSpec((bb, tt, arr.shape[2]), tok_index)
    full = lambda arr: pl.BlockSpec(arr.shape, lambda g, c, nd=arr.ndim: (0,) * nd)
    st = pl.BlockSpec((nbg,) + s0.shape[1:], state_index)
    return pl.pallas_call(
        functools.partial(_gla_body, rg=rg, seq=seq),
        grid=grid,
        in_specs=[tok(qs), tok(kin), tok(kend), tok(v), tok(gs), tok(ef), st]
                 + [full(m) for m in masks] + [full(gn)],
        out_specs=[tok(v), st],
        out_shape=[jax.ShapeDtypeStruct(v.shape, BF16), jax.ShapeDtypeStruct(s0.shape, F32)],
        scratch_shapes=[pltpu.VMEM((GLA_HEADS, ns, GLA_DV), F32)],
        compiler_params=_params(("parallel", "arbitrary")),
        name="gla",
    )(qs, kin, kend, v, gs, ef, s0, *masks, gn)


def _s5_discretise(lam_re, lam_im, log_dt, b_re, b_im):
    dt = jnp.exp(log_dt.astype(F32))[:, None]
    lr = lam_re.astype(F32)
    li = lam_im.astype(F32)
    mag = jnp.exp(lr * dt)
    abr = mag * jnp.cos(li * dt)
    abi = mag * jnp.sin(li * dt)
    den = lr * lr + li * li
    fr = ((abr - 1.0) * lr + abi * li) / den
    fi = (abi * lr - (abr - 1.0) * li) / den
    br = b_re.astype(F32)
    bi = b_im.astype(F32)
    bbr = fr[..., None] * br - fi[..., None] * bi
    bbi = fr[..., None] * bi + fi[..., None] * br
    return abr, abi, bbr, bbi


def _s5_maps(abr, abi, bbr, bbi, c_re, c_im, n_blk):
    groups, states, ch = bbr.shape
    gpb = groups // n_blk
    eye = jnp.eye(gpb, dtype=F32)

    def b_map(bb):
        return jnp.einsum("gph,gk->ghkp", bb, eye).reshape(gpb * ch, gpb * states)

    def c_map(cc):
        return jnp.einsum("ghp,gk->gpkh", cc, eye).reshape(gpb * states, gpb * ch)

    bbig, cbig, ar, ai = [], [], [], []
    for m in range(n_blk):
        gsl = slice(gpb * m, gpb * (m + 1))
        bbig.append(jnp.concatenate([b_map(bbr[gsl]), b_map(bbi[gsl])], axis=1))
        cbig.append(jnp.concatenate([c_map(c_re[gsl].astype(F32)), -c_map(c_im[gsl].astype(F32))], axis=0))
        ar.append(abr[gsl].reshape(-1))
        ai.append(abi[gsl].reshape(-1))
    return (jnp.stack(bbig).astype(BF16), jnp.stack(cbig).astype(BF16), jnp.stack(ar), jnp.stack(ai))


def _s5_state_pack(re, im, n_blk):
    b = re.shape[0]
    re = re.astype(F32).reshape(b, n_blk, -1)
    im = im.astype(F32).reshape(b, n_blk, -1)
    return jnp.stack([re, im], axis=2).reshape(b, -1)


def _s5_state_unpack(st, n_blk, groups, states):
    b = st.shape[0]
    st = st.reshape(b, n_blk, 2, -1)
    return st[:, :, 0].reshape(b, groups, states), st[:, :, 1].reshape(b, groups, states)


def _perm_matrix(nb, seq):
    rows = nb * seq
    dst = np.arange(rows)
    src = (dst % nb) * seq + dst // nb
    p = np.zeros((rows, rows), np.float32)
    p[dst, src] = 1.0
    return jnp.asarray(p, dtype=BF16), jnp.asarray(p.T, dtype=BF16)


def _cumsum_masks(grp, seq):
    tok = np.arange(grp)
    tb, tt = tok // seq, tok % seq
    same = tb[:, None] == tb[None, :]
    m = np.concatenate([same & (tt[None, :] <= tt[:, None]), same], axis=0)
    return jnp.asarray(m.astype(np.float32), dtype=BF16)


def _layer_weights(l, norm_ffn1, ffn1_gate, ffn1_up, ffn1_down, norm_mix, w_in, s5_lam_re, s5_lam_im, s5_log_dt,
                   s5_b_re, s5_b_im, s5_c_re, s5_c_im, s5_d, s5_glu_w, s5_glu_b, gla_gate_w, gla_gate_b, gla_norm,
                   w_out, norm_ffn2, ffn2_gate, ffn2_up, ffn2_down):
    d = w_in.shape[1]
    s5w = s5_d.shape[1]
    qkw = gla_gate_b.shape[1]
    main = w_in.shape[2] - GLA_GATE_RANK
    n_blk = 2
    abr, abi, bbr, bbi = _s5_discretise(s5_lam_re[l], s5_lam_im[l], s5_log_dt[l], s5_b_re[l], s5_b_im[l])
    bbig, cbig, ar, ai = _s5_maps(abr, abi, bbr, bbi, s5_c_re[l], s5_c_im[l], n_blk)
    w_a = jnp.zeros((d, LANES), F32).at[:, :GLA_GATE_RANK].set(w_in[l][:, main:])
    gate_w = jnp.zeros((LANES, qkw), F32).at[:GLA_GATE_RANK].set(gla_gate_w[l])
    row = lambda a: a.astype(F32).reshape(1, -1)
    return dict(
        norm_ffn1=row(norm_ffn1[l]), ffn1=(ffn1_gate[l].astype(BF16), ffn1_up[l].astype(BF16), ffn1_down[l].astype(BF16)),
        norm_ffn2=row(norm_ffn2[l]), ffn2=(ffn2_gate[l].astype(BF16), ffn2_up[l].astype(BF16), ffn2_down[l].astype(BF16)),
        norm_mix=row(norm_mix[l]), w_in=w_in[l][:, :main].astype(BF16), w_a=w_a.astype(BF16),
        bbig=bbig, cbig=cbig, abar_re=ar, abar_im=ai, dskip=row(s5_d[l]),
        glu_w=s5_glu_w[l].astype(BF16), glu_b=row(s5_glu_b[l]),
        gate_w=gate_w.astype(BF16), gate_b=row(gla_gate_b[l]), gla_norm=row(gla_norm[l]),
        w_out=w_out[l].astype(BF16).reshape(2, s5w, -1), n_blk=n_blk)


def _decoder(x, s5_re0, s5_im0, gla0, layers, norm_final, *, prompt):
    b, t, d = x.shape
    if prompt:
        seq = GLA_CHUNK if t % GLA_CHUNK == 0 else t
        nb = b
        view = (b, t)
        pre_blk = (b, seq)
        rg = 256
        gla_blk = (rg // seq, seq)
        gla_grid = (b // gla_blk[0], t // seq)
        tok_index = lambda g, c: (g, c, 0)
    else:
        seq = t
        nb = b
        view = (1, b * t)
        pre_blk = (1, b * t)
        rg = 128
        gla_blk = (1, rg)
        gla_grid = (b * t // rg, 1)
        tok_index = lambda g, c: (0, g, 0)
    state_index = lambda g, c: (g, 0, 0, 0)
    assert nb * seq == ROW_TILE and nb % SUBLANES == 0
    n = b * t
    perm, perm_t = _perm_matrix(nb, seq)
    mcs = _cumsum_masks(256, seq)
    gf = norm_final.astype(F32).reshape(1, -1)
    xf = x.astype(F32).reshape(n, d)
    new_re, new_im, new_gla = [], [], []
    for l, w in enumerate(layers):
        w = dict(w, perm=perm, perm_t=perm_t, mcs=mcs)
        groups, states = s5_re0.shape[2], s5_re0.shape[3]
        x1 = _ffn_call(xf, w["norm_ffn1"], *w["ffn1"], gf, final_norm=False)
        s0 = _s5_state_pack(s5_re0[l], s5_im0[l], w["n_blk"])
        zz, qs, kin, kend, v, gs, ef, sfin = _mixer_pre_call(
            x1.reshape(view + (d,)), s0, w, nb=nb, seq=seq, blk=pre_blk)
        o, s_new = _gla_call(qs, kin, kend, v, gs, ef, gla0[l].astype(F32), w["gla_norm"], rg=rg, seq=seq,
                             blk=gla_blk, grid=gla_grid, tok_index=tok_index, state_index=state_index)
        last = l == len(layers) - 1
        xf = _ffn_call(x1, w["norm_ffn2"], *w["ffn2"], gf, final_norm=last,
                       mix=(zz.reshape(n, -1), o.reshape(n, -1), w["w_out"]))
        hr, hi = _s5_state_unpack(sfin, w["n_blk"], groups, states)
        new_re.append(hr)
        new_im.append(hi)
        new_gla.append(s_new)
    return xf.reshape(b, t, d), jnp.stack(new_re), jnp.stack(new_im), jnp.stack(new_gla)


def kernel(x_prompt, x_sample, state_s5_re, state_s5_im, state_gla, norm_ffn1, ffn1_gate, ffn1_up, ffn1_down, norm_mix, w_in, s5_lam_re, s5_lam_im, s5_log_dt, s5_b_re, s5_b_im, s5_c_re, s5_c_im, s5_d, s5_glu_w, s5_glu_b, gla_gate_w, gla_gate_b, gla_norm, w_out, norm_ffn2, ffn2_gate, ffn2_up, ffn2_down, norm_final):
    depth = norm_ffn1.shape[0]
    layers = [_layer_weights(l, norm_ffn1, ffn1_gate, ffn1_up, ffn1_down, norm_mix, w_in, s5_lam_re, s5_lam_im,
                             s5_log_dt, s5_b_re, s5_b_im, s5_c_re, s5_c_im, s5_d, s5_glu_w, s5_glu_b, gla_gate_w,
                             gla_gate_b, gla_norm, w_out, norm_ffn2, ffn2_gate, ffn2_up, ffn2_down)
              for l in range(depth)]
    bp = x_prompt.shape[0]
    zero_s5 = jnp.zeros((depth, bp) + state_s5_re.shape[2:], F32)
    zero_gla = jnp.zeros((depth, bp) + state_gla.shape[2:], F32)
    y_p, p_re, p_im, p_gla = _decoder(x_prompt, zero_s5, zero_s5, zero_gla, layers, norm_final, prompt=True)
    y_s, s_re, s_im, s_gla = _decoder(x_sample, state_s5_re, state_s5_im, state_gla, layers, norm_final, prompt=False)
    return (y_p, y_s, p_re, p_im, p_gla, s_re, s_im, s_gla)
```

```python
import functools
import math

import jax
import jax.numpy as jnp
import numpy as np
from jax import lax
from jax.experimental import pallas as pl
from jax.experimental.pallas import tpu as pltpu

F32 = jnp.float32
BF16 = jnp.bfloat16

EPS = 1e-6
S5_GROUP_CH = 16
S5_STATE = 64
GLA_HEADS = 4
GLA_DK = 64
GLA_DV = 128
GLA_GATE_RANK = 16
GLA_GATE_NORM = 16.0
GLA_CHUNK = 64

V7X_VMEM_LIMIT_BYTES = 56 * 1024 * 1024
LANES = 128
SUBLANES = 8
MXU_TILE = 256
ROW_TILE = 512
S5_COL_HALF = 512


def _dot(a, b):
    return jnp.dot(a, b, preferred_element_type=F32)


def _rms(x, g):
    return x * lax.rsqrt(jnp.mean(x * x, axis=-1, keepdims=True) + EPS) * g


def _params(semantics):
    return pltpu.CompilerParams(dimension_semantics=semantics, vmem_limit_bytes=V7X_VMEM_LIMIT_BYTES)


def _ffn_body(*refs, with_mix, final_norm):
    if with_mix:
        x_ref, zz_ref, o_ref, wout_ref, g_ref, wg_ref, wu_ref, wd_ref, gf_ref, out_ref = refs
        x = x_ref[...] + _dot(zz_ref[...], wout_ref[0]) + _dot(o_ref[...], wout_ref[1])
    else:
        x_ref, g_ref, wg_ref, wu_ref, wd_ref, gf_ref, out_ref = refs
        x = x_ref[...]
    h = _rms(x, g_ref[...]).astype(BF16)
    acc = None
    for c0 in range(0, wg_ref.shape[1], MXU_TILE):
        cols = slice(c0, c0 + MXU_TILE)
        gate = _dot(h, wg_ref[:, cols])
        up = _dot(h, wu_ref[:, cols])
        act = (gate * jax.nn.sigmoid(gate) * up).astype(BF16)
        part = _dot(act, wd_ref[cols, :])
        acc = part if acc is None else acc + part
    y = x + 0.5 * acc
    if final_norm:
        y = _rms(y, gf_ref[...])
    out_ref[...] = y


def _ffn_call(x, g, wg, wu, wd, gf, *, final_norm, mix=None):
    n, d = x.shape
    dff = wg.shape[1]
    assert n % ROW_TILE == 0 and dff % MXU_TILE == 0
    tm = ROW_TILE
    row = lambda c: pl.BlockSpec((tm, c), lambda i: (i, 0))
    res = lambda arr: pl.BlockSpec(arr.shape, lambda i, nd=arr.ndim: (0,) * nd, pipeline_mode=pl.Buffered(1))
    w_specs = [res(g), res(wg), res(wu), res(wd), res(gf)]
    if mix is None:
        in_specs = [row(d)] + w_specs
        args = (x, g, wg, wu, wd, gf)
    else:
        zz, o, wout = mix
        in_specs = [row(d), row(zz.shape[1]), row(o.shape[1]), res(wout)] + w_specs
        args = (x, zz, o, wout, g, wg, wu, wd, gf)
    return pl.pallas_call(
        functools.partial(_ffn_body, with_mix=mix is not None, final_norm=final_norm),
        grid=(n // tm,),
        in_specs=in_specs,
        out_specs=row(d),
        out_shape=jax.ShapeDtypeStruct((n, d), F32),
        compiler_params=_params(("parallel",)),
        name="ffn_post" if mix is not None else "ffn",
    )(*args)


PRE_CONSTS = ("norm_mix", "w_in", "w_a", "perm", "perm_t", "bbig", "cbig", "abar_re", "abar_im", "dskip",
              "glu_w", "glu_b", "gate_w", "gate_b", "mcs")
GLA_CONSTS = ("m1", "m2", "m2t", "m3", "rt", "gla_norm")


def _split3(x):
    hi = x.astype(BF16)
    r1 = x - hi.astype(F32)
    mid = r1.astype(BF16)
    lo = (r1 - mid.astype(F32)).astype(BF16)
    return hi, mid, lo


def _s5_scan(xs_ref, st_ref, ar_ref, ai_ref, m, *, nb, seq):
    half_cols = xs_ref.shape[2] // 2
    for c0 in range(0, half_cols, S5_COL_HALF):
        re = slice(c0, c0 + S5_COL_HALF)
        im = slice(half_cols + c0, half_cols + c0 + S5_COL_HALF)
        st_re = slice(2 * half_cols * m + c0, 2 * half_cols * m + c0 + S5_COL_HALF)
        st_im = slice(2 * half_cols * m + half_cols + c0, 2 * half_cols * m + half_cols + c0 + S5_COL_HALF)
        ar = jnp.broadcast_to(ar_ref[m:m + 1, re], (SUBLANES, S5_COL_HALF))
        ai = jnp.broadcast_to(ai_ref[m:m + 1, re], (SUBLANES, S5_COL_HALF))
        for r0 in range(0, nb, SUBLANES):
            rows = slice(r0, r0 + SUBLANES)
            hr, hi = st_ref[rows, st_re], st_ref[rows, st_im]
            for t in range(seq):
                trow = slice(t * nb + r0, t * nb + r0 + SUBLANES)
                xr = xs_ref[m, trow, re]
                xi = xs_ref[m, trow, im]
                hr, hi = ar * hr - ai * hi + xr, ar * hi + ai * hr + xi
                xs_ref[m, trow, re] = hr
                xs_ref[m, trow, im] = hi
            st_ref[rows, st_re] = hr
            st_ref[rows, st_im] = hi


def _pre_compute(x, st_ref, xs_ref, c, *, nb, seq):
    rows = nb * seq
    s5w = c["dskip"].shape[-1]
    qkw = c["gate_b"].shape[-1]
    vw = (c["w_in"].shape[-1] - s5w - 2 * qkw) // 2
    h = _rms(x, c["norm_mix"][...]).astype(BF16)
    p = _dot(h, c["w_in"][...])
    a = _dot(h, c["w_a"][...])
    u = p[:, :s5w]
    q = p[:, s5w:s5w + qkw]
    k = p[:, s5w + qkw:s5w + 2 * qkw]
    v = p[:, s5w + 2 * qkw:s5w + 2 * qkw + vw]
    g = p[:, s5w + 2 * qkw + vw:]

    u_hi = u.astype(BF16)
    u_lo = (u - u_hi.astype(F32)).astype(BF16)
    perm = c["perm"][...]
    up_hi = _dot(perm, u_hi)
    up = up_hi + _dot(perm, u_lo)
    up_hi = up_hi.astype(BF16)
    n_blk = c["bbig"].shape[0]
    ch_blk = c["bbig"].shape[1]
    ys = []
    for m in range(n_blk):
        xs_ref[m] = _dot(up_hi[:, ch_blk * m:ch_blk * (m + 1)], c["bbig"][m])
        _s5_scan(xs_ref, st_ref, c["abar_re"], c["abar_im"], m, nb=nb, seq=seq)
        hs = xs_ref[m].astype(BF16)
        half = rows // 2
        ys.append(jnp.concatenate([_dot(hs[:half], c["cbig"][m]), _dot(hs[half:], c["cbig"][m])], axis=0))
    y = jnp.concatenate(ys, axis=1) + c["dskip"][...] * up
    z = jax.nn.gelu(y)
    z = z * jax.nn.sigmoid(_dot(z.astype(BF16), c["glu_w"][...]) + c["glu_b"][...])
    zz = _dot(c["perm_t"][...], z.astype(BF16)).astype(BF16)

    gate = _dot(a.astype(BF16), c["gate_w"][...]) + c["gate_b"][...]
    log_f = (jnp.minimum(gate, 0.0) - jnp.log(1.0 + jnp.exp(-jnp.abs(gate)))) / GLA_GATE_NORM
    pieces = _split3(log_f)
    mcs = c["mcs"][...]
    grp = mcs.shape[1]
    bcum, blast = [], []
    for r0 in range(0, rows, grp):
        cs = [_dot(mcs, piece[r0:r0 + grp]) for piece in pieces]
        tot = (cs[0] + cs[1]) + cs[2]
        bcum.append(tot[:grp])
        blast.append(tot[grp:])
    bcum = jnp.concatenate(bcum, axis=0)
    blast = jnp.concatenate(blast, axis=0)
    qs = q * (GLA_DK ** -0.5) * jnp.exp(bcum)
    kin = k * jnp.exp(-bcum)
    kend = k * jnp.exp(blast - bcum)
    return zz, qs, kin, kend, v, g * jax.nn.sigmoid(g), jnp.exp(blast)


def _gla_group(qs, kin, kend, v, gs, ef, s_view, c, *, rg, seq):
    nbg = rg // seq
    qkw = GLA_HEADS * GLA_DK
    kend_t = jnp.transpose(kend)
    ef_t = jnp.transpose(ef)
    lane_head = lax.broadcasted_iota(jnp.int32, (rg, qkw), 1) // GLA_DK
    m1 = c["m1"][...] > 0
    m2 = c["m2"][...] > 0
    m2t = c["m2t"][...] > 0
    m3 = c["m3"][...] > 0
    rt = c["rt"][...]
    gn = c["gla_norm"][...]
    outs = []
    for hd in range(GLA_HEADS):
        q_h = jnp.where(lane_head == hd, qs, 0.0).astype(BF16)
        att = lax.dot_general(q_h, kin, (((1,), (1,)), ((), ())), preferred_element_type=F32)
        att = jnp.where(m1, att, 0.0).astype(BF16)
        v_h = v[:, GLA_DV * hd:GLA_DV * (hd + 1)]
        q_exp = jnp.where(m2, _dot(q_h, rt), 0.0).astype(BF16)
        s_h = s_view[hd]
        o = _dot(att, v_h) + _dot(q_exp, s_h.astype(BF16))
        k_t = jnp.concatenate([kend_t[GLA_DK * hd:GLA_DK * (hd + 1)]] * nbg, axis=0)
        k_exp_t = jnp.where(m2t, k_t, 0.0).astype(BF16)
        upd = _dot(k_exp_t, v_h)
        e_t = jnp.concatenate([ef_t[GLA_DK * hd:GLA_DK * (hd + 1)]] * nbg, axis=0)
        decay = jnp.sum(jnp.where(m3, e_t, 0.0), axis=1, keepdims=True)
        s_view[hd] = decay * s_h + upd
        o = o * lax.rsqrt(jnp.mean(o * o, axis=-1, keepdims=True) + EPS) * gn
        outs.append((o * gs[:, GLA_DV * hd:GLA_DV * (hd + 1)]).astype(BF16))
    return jnp.concatenate(outs, axis=1)


def _gla_state_load(s_view, s0_ref, b0, nbg):
    for hd in range(GLA_HEADS):
        for b in range(nbg):
            s_view[hd, GLA_DK * b:GLA_DK * (b + 1), :] = s0_ref[b0 + b, hd]


def _gla_state_store(sout_ref, s_view, b0, nbg):
    for hd in range(GLA_HEADS):
        for b in range(nbg):
            sout_ref[b0 + b, hd] = s_view[hd, GLA_DK * b:GLA_DK * (b + 1), :]


def _mixer_fused_body(*refs, nb, seq, rg):
    n_pre, n_gla = len(PRE_CONSTS), len(GLA_CONSTS)
    x_ref, s0_ref, g0_ref = refs[:3]
    c = dict(zip(PRE_CONSTS + GLA_CONSTS, refs[3:3 + n_pre + n_gla]))
    zz_ref, o_ref, sfin_ref, gfin_ref, st_ref, xs_ref, gst_ref = refs[3 + n_pre + n_gla:]
    rows = nb * seq
    nbg = rg // seq
    step = pl.program_id(0)

    @pl.when(step == 0)
    def _():
        st_ref[...] = s0_ref[...]
        for gi in range(rows // rg):
            _gla_state_load(gst_ref.at[gi], g0_ref, gi * nbg, nbg)

    x = x_ref[...].reshape(rows, x_ref.shape[-1])
    zz, qs, kin, kend, v, gs, ef = _pre_compute(x, st_ref, xs_ref, c, nb=nb, seq=seq)
    kin = kin.astype(BF16)
    v = v.astype(BF16)
    os_ = []
    for gi in range(rows // rg):
        r = slice(gi * rg, (gi + 1) * rg)
        os_.append(_gla_group(qs[r], kin[r], kend[r], v[r], gs[r], ef[r], gst_ref.at[gi], c, rg=rg, seq=seq))
    zz_ref[...] = zz.reshape(zz_ref.shape)
    o_ref[...] = jnp.concatenate(os_, axis=0).reshape(o_ref.shape)
    sfin_ref[...] = st_ref[...]

    @pl.when(step == pl.num_programs(0) - 1)
    def _():
        for gi in range(rows // rg):
            _gla_state_store(gfin_ref, gst_ref.at[gi], gi * nbg, nbg)


def _mixer_fused_call(x3, s0, g0, w, gm, *, nb, seq, rg):
    b, t, d = x3.shape
    rows = nb * seq
    assert b == nb and t % seq == 0 and rows % rg == 0
    s5w = w["dskip"].shape[1]
    vw = g0.shape[1] * g0.shape[3]
    tok = lambda cdim: pl.BlockSpec((nb, seq, cdim), lambda i: (0, i, 0))
    full = lambda arr: pl.BlockSpec(arr.shape, lambda i, nd=arr.ndim: (0,) * nd)
    consts = tuple(w[k] for k in PRE_CONSTS) + tuple(gm[k] for k in GLA_CONSTS)
    ns = (rg // seq) * GLA_DK
    return pl.pallas_call(
        functools.partial(_mixer_fused_body, nb=nb, seq=seq, rg=rg),
        grid=(t // seq,),
        in_specs=[tok(d), full(s0), full(g0)] + [full(a) for a in consts],
        out_specs=[tok(s5w), tok(vw), full(s0), full(g0)],
        out_shape=[jax.ShapeDtypeStruct((b, t, s5w), BF16), jax.ShapeDtypeStruct((b, t, vw), BF16),
                   jax.ShapeDtypeStruct(s0.shape, F32), jax.ShapeDtypeStruct(g0.shape, F32)],
        scratch_shapes=[pltpu.VMEM(s0.shape, F32),
                        pltpu.VMEM((w["bbig"].shape[0], rows, w["bbig"].shape[2]), F32),
                        pltpu.VMEM((rows // rg, GLA_HEADS, ns, GLA_DV), F32)],
        compiler_params=_params(("arbitrary",)),
        name="mixer",
    )(x3, s0, g0, *consts)


def _mixer_pre_body(*refs, nb, seq):
    n_pre = len(PRE_CONSTS)
    x_ref, s0_ref = refs[:2]
    c = dict(zip(PRE_CONSTS, refs[2:2 + n_pre]))
    zz_ref, qs_ref, kin_ref, kend_ref, v_ref, gs_ref, ef_ref, sfin_ref, st_ref, xs_ref = refs[2 + n_pre:]

    @pl.when(pl.program_id(0) == 0)
    def _():
        st_ref[...] = s0_ref[...]

    x = x_ref[...].reshape(nb * seq, x_ref.shape[-1])
    zz, qs, kin, kend, v, gs, ef = _pre_compute(x, st_ref, xs_ref, c, nb=nb, seq=seq)
    zz_ref[...] = zz.reshape(zz_ref.shape)
    sfin_ref[...] = st_ref[...]
    qs_ref[...] = qs.astype(BF16).reshape(qs_ref.shape)
    kin_ref[...] = kin.astype(BF16).reshape(kin_ref.shape)
    kend_ref[...] = kend.astype(BF16).reshape(kend_ref.shape)
    v_ref[...] = v.astype(BF16).reshape(v_ref.shape)
    gs_ref[...] = gs.reshape(gs_ref.shape)
    ef_ref[...] = ef.reshape(ef_ref.shape)


def _mixer_pre_call(x3, s0, w, *, nb, seq, blk):
    bv, tv, d = x3.shape
    bb, tt = blk
    rows = nb * seq
    assert bb * tt == rows and bv == bb and tv % tt == 0
    s5w = w["dskip"].shape[1]
    qkw = w["gate_b"].shape[1]
    vw = (w["w_in"].shape[1] - s5w - 2 * qkw) // 2
    tok = lambda cdim: pl.BlockSpec((bb, tt, cdim), lambda i: (0, i, 0))
    full = lambda arr: pl.BlockSpec(arr.shape, lambda i, nd=arr.ndim: (0,) * nd)
    consts = tuple(w[k] for k in PRE_CONSTS)
    tok_shape = lambda cdim, dt: jax.ShapeDtypeStruct((bv, tv, cdim), dt)
    return pl.pallas_call(
        functools.partial(_mixer_pre_body, nb=nb, seq=seq),
        grid=(tv // tt,),
        in_specs=[tok(d), full(s0)] + [full(a) for a in consts],
        out_specs=[tok(s5w), tok(qkw), tok(qkw), tok(qkw), tok(vw), tok(vw), tok(qkw), full(s0)],
        out_shape=[tok_shape(s5w, BF16), tok_shape(qkw, BF16), tok_shape(qkw, BF16), tok_shape(qkw, BF16),
                   tok_shape(vw, BF16), tok_shape(vw, F32), tok_shape(qkw, F32),
                   jax.ShapeDtypeStruct(s0.shape, F32)],
        scratch_shapes=[pltpu.VMEM(s0.shape, F32),
                        pltpu.VMEM((w["bbig"].shape[0], rows, w["bbig"].shape[2]), F32)],
        compiler_params=_params(("arbitrary",)),
        name="mixer_pre",
    )(x3, s0, *consts)


def _gla_body(*refs, rg, seq):
    qs_ref, kin_ref, kend_ref, v_ref, gs_ref, ef_ref, s0_ref = refs[:7]
    c = dict(zip(GLA_CONSTS, refs[7:7 + len(GLA_CONSTS)]))
    o_ref, sout_ref, s_ref = refs[7 + len(GLA_CONSTS):]
    nbg = rg // seq
    chunk = pl.program_id(1)

    @pl.when(chunk == 0)
    def _():
        _gla_state_load(s_ref, s0_ref, 0, nbg)

    flat = lambda ref: ref[...].reshape(rg, ref.shape[-1])
    o = _gla_group(flat(qs_ref).astype(F32), flat(kin_ref), flat(kend_ref).astype(F32), flat(v_ref),
                   flat(gs_ref), flat(ef_ref), s_ref, c, rg=rg, seq=seq)
    o_ref[...] = o.reshape(o_ref.shape)

    @pl.when(chunk == pl.num_programs(1) - 1)
    def _():
        _gla_state_store(sout_ref, s_ref, 0, nbg)


def _gla_masks(rg, seq):
    nbg = rg // seq
    ns = nbg * GLA_DK
    tok = np.arange(rg)
    tb, tt = tok // seq, tok % seq
    srow = np.arange(ns)
    sb, sd = srow // GLA_DK, srow % GLA_DK
    m1 = (tb[:, None] == tb[None, :]) & (tt[None, :] <= tt[:, None])
    m2 = tb[:, None] == sb[None, :]
    m3 = (sb[:, None] * seq) == tok[None, :]
    rt = (np.arange(GLA_HEADS * GLA_DK)[:, None] % GLA_DK) == sd[None, :]
    as_f32 = lambda a: jnp.asarray(a.astype(np.float32))
    return dict(m1=as_f32(m1), m2=as_f32(m2), m2t=as_f32(m2.T), m3=as_f32(m3),
                rt=jnp.asarray(rt.astype(np.float32), dtype=BF16))


def _gla_call(qs, kin, kend, v, gs, ef, s0, gm, *, rg, seq, blk, grid, tok_index, state_index):
    bb, tt = blk
    assert bb * tt == rg and rg % seq == 0
    nbg = rg // seq
    ns = nbg * GLA_DK
    consts = tuple(gm[k] for k in GLA_CONSTS)
    tok = lambda arr: pl.BlockSpec((bb, tt, arr.shape[2]), tok_index)
    full = lambda arr: pl.BlockSpec(arr.shape, lambda g, c, nd=arr.ndim: (0,) * nd)
    st = pl.BlockSpec((nbg,) + s0.shape[1:], state_index)
    return pl.pallas_call(
        functools.partial(_gla_body, rg=rg, seq=seq),
        grid=grid,
        in_specs=[tok(qs), tok(kin), tok(kend), tok(v), tok(gs), tok(ef), st] + [full(a) for a in consts],
        out_specs=[tok(v), st],
        out_shape=[jax.ShapeDtypeStruct(v.shape, BF16), jax.ShapeDtypeStruct(s0.shape, F32)],
        scratch_shapes=[pltpu.VMEM((GLA_HEADS, ns, GLA_DV), F32)],
        compiler_params=_params(("parallel", "arbitrary")),
        name="gla",
    )(qs, kin, kend, v, gs, ef, s0, *consts)


def _s5_discretise(lam_re, lam_im, log_dt, b_re, b_im):
    dt = jnp.exp(log_dt.astype(F32))[:, None]
    lr = lam_re.astype(F32)
    li = lam_im.astype(F32)
    mag = jnp.exp(lr * dt)
    abr = mag * jnp.cos(li * dt)
    abi = mag * jnp.sin(li * dt)
    den = lr * lr + li * li
    fr = ((abr - 1.0) * lr + abi * li) / den
    fi = (abi * lr - (abr - 1.0) * li) / den
    br = b_re.astype(F32)
    bi = b_im.astype(F32)
    bbr = fr[..., None] * br - fi[..., None] * bi
    bbi = fr[..., None] * bi + fi[..., None] * br
    return abr, abi, bbr, bbi


def _s5_maps(abr, abi, bbr, bbi, c_re, c_im, n_blk):
    groups, states, ch = bbr.shape
    gpb = groups // n_blk
    eye = jnp.eye(gpb, dtype=F32)

    def b_map(bb):
        return jnp.einsum("gph,gk->ghkp", bb, eye).reshape(gpb * ch, gpb * states)

    def c_map(cc):
        return jnp.einsum("ghp,gk->gpkh", cc, eye).reshape(gpb * states, gpb * ch)

    bbig, cbig, ar, ai = [], [], [], []
    for m in range(n_blk):
        gsl = slice(gpb * m, gpb * (m + 1))
        bbig.append(jnp.concatenate([b_map(bbr[gsl]), b_map(bbi[gsl])], axis=1))
        cbig.append(jnp.concatenate([c_map(c_re[gsl].astype(F32)), -c_map(c_im[gsl].astype(F32))], axis=0))
        ar.append(abr[gsl].reshape(-1))
        ai.append(abi[gsl].reshape(-1))
    return (jnp.stack(bbig).astype(BF16), jnp.stack(cbig).astype(BF16), jnp.stack(ar), jnp.stack(ai))


def _s5_state_pack(re, im, n_blk):
    b = re.shape[0]
    re = re.astype(F32).reshape(b, n_blk, -1)
    im = im.astype(F32).reshape(b, n_blk, -1)
    return jnp.stack([re, im], axis=2).reshape(b, -1)


def _s5_state_unpack(st, n_blk, groups, states):
    b = st.shape[0]
    st = st.reshape(b, n_blk, 2, -1)
    return st[:, :, 0].reshape(b, groups, states), st[:, :, 1].reshape(b, groups, states)


def _perm_matrix(nb, seq):
    rows = nb * seq
    dst = np.arange(rows)
    src = (dst % nb) * seq + dst // nb
    p = np.zeros((rows, rows), np.float32)
    p[dst, src] = 1.0
    return jnp.asarray(p, dtype=BF16), jnp.asarray(p.T, dtype=BF16)


def _cumsum_masks(grp, seq):
    tok = np.arange(grp)
    tb, tt = tok // seq, tok % seq
    same = tb[:, None] == tb[None, :]
    m = np.concatenate([same & (tt[None, :] <= tt[:, None]), same], axis=0)
    return jnp.asarray(m.astype(np.float32), dtype=BF16)


def _layer_weights(l, norm_ffn1, ffn1_gate, ffn1_up, ffn1_down, norm_mix, w_in, s5_lam_re, s5_lam_im, s5_log_dt,
                   s5_b_re, s5_b_im, s5_c_re, s5_c_im, s5_d, s5_glu_w, s5_glu_b, gla_gate_w, gla_gate_b, gla_norm,
                   w_out, norm_ffn2, ffn2_gate, ffn2_up, ffn2_down):
    d = w_in.shape[1]
    s5w = s5_d.shape[1]
    qkw = gla_gate_b.shape[1]
    main = w_in.shape[2] - GLA_GATE_RANK
    n_blk = 2
    abr, abi, bbr, bbi = _s5_discretise(s5_lam_re[l], s5_lam_im[l], s5_log_dt[l], s5_b_re[l], s5_b_im[l])
    bbig, cbig, ar, ai = _s5_maps(abr, abi, bbr, bbi, s5_c_re[l], s5_c_im[l], n_blk)
    w_a = jnp.zeros((d, LANES), F32).at[:, :GLA_GATE_RANK].set(w_in[l][:, main:])
    gate_w = jnp.zeros((LANES, qkw), F32).at[:GLA_GATE_RANK].set(gla_gate_w[l])
    row = lambda a: a.astype(F32).reshape(1, -1)
    return dict(
        norm_ffn1=row(norm_ffn1[l]), ffn1=(ffn1_gate[l].astype(BF16), ffn1_up[l].astype(BF16), ffn1_down[l].astype(BF16)),
        norm_ffn2=row(norm_ffn2[l]), ffn2=(ffn2_gate[l].astype(BF16), ffn2_up[l].astype(BF16), ffn2_down[l].astype(BF16)),
        norm_mix=row(norm_mix[l]), w_in=w_in[l][:, :main].astype(BF16), w_a=w_a.astype(BF16),
        bbig=bbig, cbig=cbig, abar_re=ar, abar_im=ai, dskip=row(s5_d[l]),
        glu_w=s5_glu_w[l].astype(BF16), glu_b=row(s5_glu_b[l]),
        gate_w=gate_w.astype(BF16), gate_b=row(gla_gate_b[l]), gla_norm=row(gla_norm[l]),
        w_out=w_out[l].astype(BF16).reshape(2, s5w, -1), n_blk=n_blk)


def _decoder(x, s5_re0, s5_im0, gla0, layers, norm_final, *, prompt):
    b, t, d = x.shape
    seq = (GLA_CHUNK if t % GLA_CHUNK == 0 else t) if prompt else t
    assert b * seq == ROW_TILE and b % SUBLANES == 0
    rg = 256 if prompt else 128
    n = b * t
    perm, perm_t = _perm_matrix(b, seq)
    mcs = _cumsum_masks(256, seq)
    gmasks = _gla_masks(rg, seq)
    gf = norm_final.astype(F32).reshape(1, -1)
    xf = x.astype(F32).reshape(n, d)
    new_re, new_im, new_gla = [], [], []
    for l, w in enumerate(layers):
        w = dict(w, perm=perm, perm_t=perm_t, mcs=mcs)
        gm = dict(gmasks, gla_norm=w["gla_norm"])
        groups, states = s5_re0.shape[2], s5_re0.shape[3]
        x1 = _ffn_call(xf, w["norm_ffn1"], *w["ffn1"], gf, final_norm=False)
        s0 = _s5_state_pack(s5_re0[l], s5_im0[l], w["n_blk"])
        g0 = gla0[l].astype(F32)
        if prompt:
            zz, o, sfin, s_new = _mixer_fused_call(x1.reshape(b, t, d), s0, g0, w, gm, nb=b, seq=seq, rg=rg)
        else:
            zz, qs, kin, kend, v, gs, ef, sfin = _mixer_pre_call(
                x1.reshape(1, n, d), s0, w, nb=b, seq=seq, blk=(1, n))
            o, s_new = _gla_call(qs, kin, kend, v, gs, ef, g0, gm, rg=rg, seq=seq, blk=(1, rg), grid=(n // rg, 1),
                                 tok_index=lambda g, c: (0, g, 0), state_index=lambda g, c: (g, 0, 0, 0))
        last = l == len(layers) - 1
        xf = _ffn_call(x1, w["norm_ffn2"], *w["ffn2"], gf, final_norm=last,
                       mix=(zz.reshape(n, -1), o.reshape(n, -1), w["w_out"]))
        hr, hi = _s5_state_unpack(sfin, w["n_blk"], groups, states)
        new_re.append(hr)
        new_im.append(hi)
        new_gla.append(s_new)
    return xf.reshape(b, t, d), jnp.stack(new_re), jnp.stack(new_im), jnp.stack(new_gla)


def kernel(x_prompt, x_sample, state_s5_re, state_s5_im, state_gla, norm_ffn1, ffn1_gate, ffn1_up, ffn1_down, norm_mix, w_in, s5_lam_re, s5_lam_im, s5_log_dt, s5_b_re, s5_b_im, s5_c_re, s5_c_im, s5_d, s5_glu_w, s5_glu_b, gla_gate_w, gla_gate_b, gla_norm, w_out, norm_ffn2, ffn2_gate, ffn2_up, ffn2_down, norm_final):
    depth = norm_ffn1.shape[0]
    layers = [_layer_weights(l, norm_ffn1, ffn1_gate, ffn1_up, ffn1_down, norm_mix, w_in, s5_lam_re, s5_lam_im,
                             s5_log_dt, s5_b_re, s5_b_im, s5_c_re, s5_c_im, s5_d, s5_glu_w, s5_glu_b, gla_gate_w,
                             gla_gate_b, gla_norm, w_out, norm_ffn2, ffn2_gate, ffn2_up, ffn2_down)
              for l in range(depth)]
    bp = x_prompt.shape[0]
    zero_s5 = jnp.zeros((depth, bp) + state_s5_re.shape[2:], F32)
    zero_gla = jnp.zeros((depth, bp) + state_gla.shape[2:], F32)
    y_p, p_re, p_im, p_gla = _decoder(x_prompt, zero_s5, zero_s5, zero_gla, layers, norm_final, prompt=True)
    y_s, s_re, s_im, s_gla = _decoder(x_sample, state_s5_re, state_s5_im, state_gla, layers, norm_final, prompt=False)
    return (y_p, y_s, p_re, p_im, p_gla, s_re, s_im, s_gla)
```

```python
import functools
import math

import jax
import jax.numpy as jnp
import numpy as np
from jax import lax
from jax.experimental import pallas as pl
from jax.experimental.pallas import tpu as pltpu

F32 = jnp.float32
BF16 = jnp.bfloat16

EPS = 1e-6
S5_GROUP_CH = 16
S5_STATE = 64
GLA_HEADS = 4
GLA_DK = 64
GLA_DV = 128
GLA_GATE_RANK = 16
GLA_GATE_NORM = 16.0
GLA_CHUNK = 64

V7X_VMEM_LIMIT_BYTES = 56 * 1024 * 1024
LANES = 128
SUBLANES = 8
MXU_TILE = 256
ROW_TILE = 512
S5_COL_HALF = 512
MIXER_CHUNKS_PER_STEP = 1


def _dot(a, b):
    return jnp.dot(a, b, preferred_element_type=F32)


def _dot_narrow(a, b):
    half = a.shape[0] // 2
    return jnp.concatenate([_dot(a[:half], b), _dot(a[half:], b)], axis=0)


def _rms(x, g):
    return x * lax.rsqrt(jnp.mean(x * x, axis=-1, keepdims=True) + EPS) * g


def _params(semantics):
    return pltpu.CompilerParams(dimension_semantics=semantics, vmem_limit_bytes=V7X_VMEM_LIMIT_BYTES)


def _ffn_body(*refs, with_mix, final_norm):
    if with_mix:
        x_ref, zz_ref, o_ref, wout_ref, g_ref, wg_ref, wu_ref, wd_ref, gf_ref, out_ref = refs
        x = x_ref[...] + _dot(zz_ref[...], wout_ref[0]) + _dot(o_ref[...], wout_ref[1])
    else:
        x_ref, g_ref, wg_ref, wu_ref, wd_ref, gf_ref, out_ref = refs
        x = x_ref[...]
    h = _rms(x, g_ref[...]).astype(BF16)
    acc = None
    for c0 in range(0, wg_ref.shape[1], MXU_TILE):
        cols = slice(c0, c0 + MXU_TILE)
        gate = _dot(h, wg_ref[:, cols])
        up = _dot(h, wu_ref[:, cols])
        act = (gate * jax.nn.sigmoid(gate) * up).astype(BF16)
        part = _dot(act, wd_ref[cols, :])
        acc = part if acc is None else acc + part
    y = x + 0.5 * acc
    if final_norm:
        y = _rms(y, gf_ref[...])
    out_ref[...] = y


def _ffn_call(x, g, wg, wu, wd, gf, *, final_norm, mix=None):
    n, d = x.shape
    dff = wg.shape[1]
    assert n % ROW_TILE == 0 and dff % MXU_TILE == 0
    tm = ROW_TILE
    row = lambda c: pl.BlockSpec((tm, c), lambda i: (i, 0))
    res = lambda arr: pl.BlockSpec(arr.shape, lambda i, nd=arr.ndim: (0,) * nd, pipeline_mode=pl.Buffered(1))
    w_specs = [res(g), res(wg), res(wu), res(wd), res(gf)]
    if mix is None:
        in_specs = [row(d)] + w_specs
        args = (x, g, wg, wu, wd, gf)
    else:
        zz, o, wout = mix
        in_specs = [row(d), row(zz.shape[1]), row(o.shape[1]), res(wout)] + w_specs
        args = (x, zz, o, wout, g, wg, wu, wd, gf)
    return pl.pallas_call(
        functools.partial(_ffn_body, with_mix=mix is not None, final_norm=final_norm),
        grid=(n // tm,),
        in_specs=in_specs,
        out_specs=row(d),
        out_shape=jax.ShapeDtypeStruct((n, d), F32),
        compiler_params=_params(("parallel",)),
        name="ffn_post" if mix is not None else "ffn",
    )(*args)


PRE_CONSTS = ("norm_mix", "w_in", "w_a", "perm", "perm_t", "bbig", "cbig", "abar_re", "abar_im", "dskip",
              "glu_w", "glu_b", "gate_w", "gate_b", "mcs")
GLA_CONSTS = ("m1", "m2", "m2t", "m3", "gla_norm")


def _split3(x):
    hi = x.astype(BF16)
    r1 = x - hi.astype(F32)
    mid = r1.astype(BF16)
    lo = (r1 - mid.astype(F32)).astype(BF16)
    return hi, mid, lo


def _s5_scan_piece(xs_ref, st_ref, ar_ref, ai_ref, m, c0, *, nb, seq):
    half_cols = xs_ref.shape[2] // 2
    re = slice(c0, c0 + S5_COL_HALF)
    im = slice(half_cols + c0, half_cols + c0 + S5_COL_HALF)
    st_re = slice(2 * half_cols * m + c0, 2 * half_cols * m + c0 + S5_COL_HALF)
    st_im = slice(2 * half_cols * m + half_cols + c0, 2 * half_cols * m + half_cols + c0 + S5_COL_HALF)
    ar = jnp.broadcast_to(ar_ref[m:m + 1, re], (SUBLANES, S5_COL_HALF))
    ai = jnp.broadcast_to(ai_ref[m:m + 1, re], (SUBLANES, S5_COL_HALF))
    for r0 in range(0, nb, SUBLANES):
        rows = slice(r0, r0 + SUBLANES)
        hr, hi = st_ref[rows, st_re], st_ref[rows, st_im]
        for t in range(seq):
            trow = slice(t * nb + r0, t * nb + r0 + SUBLANES)
            xr = xs_ref[m, trow, re]
            xi = xs_ref[m, trow, im]
            hr, hi = ar * hr - ai * hi + xr, ar * hi + ai * hr + xi
            xs_ref[m, trow, re] = hr
            xs_ref[m, trow, im] = hi
        st_ref[rows, st_re] = hr
        st_ref[rows, st_im] = hi


def _pre_compute(x, st_ref, xs_ref, c, *, nb, seq, gla_work=None):
    rows = nb * seq
    s5w = c["dskip"].shape[-1]
    qkw = c["gate_b"].shape[-1]
    vw = (c["w_in"].shape[-1] - s5w - 2 * qkw) // 2
    h = _rms(x, c["norm_mix"][...]).astype(BF16)
    p = _dot(h, c["w_in"][...])
    a = _dot(h, c["w_a"][...])
    u = p[:, :s5w]
    q = p[:, s5w:s5w + qkw]
    k = p[:, s5w + qkw:s5w + 2 * qkw]
    v = p[:, s5w + 2 * qkw:s5w + 2 * qkw + vw]
    g = p[:, s5w + 2 * qkw + vw:]

    gate = _dot(a.astype(BF16), c["gate_w"][...]) + c["gate_b"][...]
    log_f = (jnp.minimum(gate, 0.0) - jnp.log(1.0 + jnp.exp(-jnp.abs(gate)))) / GLA_GATE_NORM
    pieces = jnp.concatenate(_split3(log_f), axis=1)
    mcs = c["mcs"][...]
    grp = mcs.shape[1]
    bcum, blast = [], []
    for r0 in range(0, rows, grp):
        cs = _dot(mcs, pieces[r0:r0 + grp])
        tot = (cs[:, :qkw] + cs[:, qkw:2 * qkw]) + cs[:, 2 * qkw:]
        bcum.append(tot[:grp])
        if mcs.shape[0] == grp:
            b3 = tot.reshape(grp // seq, seq, qkw)
            blast.append(jnp.broadcast_to(b3[:, seq - 1:seq, :], b3.shape).reshape(grp, qkw))
        else:
            blast.append(tot[grp:])
    bcum = jnp.concatenate(bcum, axis=0)
    blast = jnp.concatenate(blast, axis=0)
    qs = q * (GLA_DK ** -0.5) * jnp.exp(bcum)
    kin = k * jnp.exp(-bcum)
    kend = k * jnp.exp(blast - bcum)
    gla_q = (qs, kin, kend, v, g * jax.nn.sigmoid(g), jnp.exp(blast))
    thunks = list(gla_work(*gla_q)) if gla_work is not None else []

    u_hi = u.astype(BF16)
    u_lo = (u - u_hi.astype(F32)).astype(BF16)
    perm = c["perm"][...]
    up_hi = _dot(perm, u_hi)
    up = up_hi + _dot(perm, u_lo)
    up_hi = up_hi.astype(BF16)
    n_blk = c["bbig"].shape[0]
    ch_blk = c["bbig"].shape[1]
    for m in range(n_blk):
        xs_ref[m] = _dot(up_hi[:, ch_blk * m:ch_blk * (m + 1)], c["bbig"][m])
    scan_pieces = [(m, c0) for m in range(n_blk) for c0 in range(0, xs_ref.shape[2] // 2, S5_COL_HALF)]
    per_piece = -(-len(thunks) // len(scan_pieces))
    for i, (m, c0) in enumerate(scan_pieces):
        _s5_scan_piece(xs_ref, st_ref, c["abar_re"], c["abar_im"], m, c0, nb=nb, seq=seq)
        for thunk in thunks[i * per_piece:(i + 1) * per_piece]:
            thunk()
    ys = []
    for m in range(n_blk):
        ys.append(_dot_narrow(xs_ref[m].astype(BF16), c["cbig"][m]))
    y = jnp.concatenate(ys, axis=1) + c["dskip"][...] * up
    z = jax.nn.gelu(y)
    z = z * jax.nn.sigmoid(_dot(z.astype(BF16), c["glu_w"][...]) + c["glu_b"][...])
    zz = _dot(c["perm_t"][...], z.astype(BF16)).astype(BF16)
    return (zz,) + gla_q


def _gla_group_heads(qs, kin, kend, v, gs, ef, s_view, c, outs, *, rg, seq):
    nbg = rg // seq
    qkw = GLA_HEADS * GLA_DK
    kend_t = jnp.transpose(kend)
    ef_t = jnp.transpose(ef)
    lane_head = lax.broadcasted_iota(jnp.int32, (rg, qkw), 1) // GLA_DK
    m1 = c["m1"][...] > 0
    m2 = c["m2"][...] > 0
    m2t = c["m2t"][...] > 0
    m3 = c["m3"][...] > 0
    gn = c["gla_norm"][...]

    def head(hd):
        q_h = jnp.where(lane_head == hd, qs, 0.0)
        att = lax.dot_general(q_h.astype(BF16), kin, (((1,), (1,)), ((), ())), preferred_element_type=F32)
        att = jnp.where(m1, att, 0.0).astype(BF16)
        v_h = v[:, GLA_DV * hd:GLA_DV * (hd + 1)]
        q_rep = q_h + pltpu.roll(q_h, qkw // 2, 1)
        q_rep = q_rep + pltpu.roll(q_rep, GLA_DK, 1)
        q_exp = jnp.where(m2, jnp.concatenate([q_rep] * (nbg * GLA_DK // qkw), axis=1), 0.0).astype(BF16)
        s_h = s_view[hd]
        o = _dot(att, v_h) + _dot(q_exp, s_h.astype(BF16))
        k_t = jnp.concatenate([kend_t[GLA_DK * hd:GLA_DK * (hd + 1)]] * nbg, axis=0)
        k_exp_t = jnp.where(m2t, k_t, 0.0).astype(BF16)
        upd = _dot(k_exp_t, v_h)
        e_t = jnp.concatenate([ef_t[GLA_DK * hd:GLA_DK * (hd + 1)]] * nbg, axis=0)
        decay = jnp.sum(jnp.where(m3, e_t, 0.0), axis=1, keepdims=True)
        s_view[hd] = decay * s_h + upd
        o = o * lax.rsqrt(jnp.mean(o * o, axis=-1, keepdims=True) + EPS) * gn
        outs.append((o * gs[:, GLA_DV * hd:GLA_DV * (hd + 1)]).astype(BF16))

    return [functools.partial(head, hd) for hd in range(GLA_HEADS)]


def _gla_state_load(s_view, s0_ref, b0, nbg):
    for hd in range(GLA_HEADS):
        for b in range(nbg):
            s_view[hd, GLA_DK * b:GLA_DK * (b + 1), :] = s0_ref[b0 + b, hd]


def _gla_state_store(sout_ref, s_view, b0, nbg):
    for hd in range(GLA_HEADS):
        for b in range(nbg):
            sout_ref[b0 + b, hd] = s_view[hd, GLA_DK * b:GLA_DK * (b + 1), :]


def _mixer_fused_body(*refs, nb, seq, rg, n_sub):
    n_pre, n_gla = len(PRE_CONSTS), len(GLA_CONSTS)
    x_ref, s0_ref, g0_ref = refs[:3]
    c = dict(zip(PRE_CONSTS + GLA_CONSTS, refs[3:3 + n_pre + n_gla]))
    zz_ref, o_ref, sfin_ref, gfin_ref, st_ref, xs_ref, gst_ref = refs[3 + n_pre + n_gla:]
    rows = nb * seq
    nbg = rg // seq
    step = pl.program_id(0)

    @pl.when(step == 0)
    def _():
        st_ref[...] = s0_ref[...]
        for gi in range(rows // rg):
            _gla_state_load(gst_ref.at[gi], g0_ref, gi * nbg, nbg)

    for j in range(n_sub):
        tsl = slice(j * seq, (j + 1) * seq)
        x = x_ref[:, tsl, :].reshape(rows, x_ref.shape[-1])
        outs = [[] for _ in range(rows // rg)]

        def gla_work(qs, kin, kend, v, gs, ef, outs=outs):
            kin = kin.astype(BF16)
            v = v.astype(BF16)
            thunks = []
            for gi in range(rows // rg):
                r = slice(gi * rg, (gi + 1) * rg)
                thunks += _gla_group_heads(qs[r], kin[r], kend[r], v[r], gs[r], ef[r], gst_ref.at[gi], c,
                                           outs[gi], rg=rg, seq=seq)
            return thunks

        zz = _pre_compute(x, st_ref, xs_ref, c, nb=nb, seq=seq, gla_work=gla_work)[0]
        o = jnp.concatenate([jnp.concatenate(heads, axis=1) for heads in outs], axis=0)
        zz_ref[:, tsl, :] = zz.reshape(nb, seq, zz.shape[-1])
        o_ref[:, tsl, :] = o.reshape(nb, seq, o_ref.shape[-1])
    sfin_ref[...] = st_ref[...]

    @pl.when(step == pl.num_programs(0) - 1)
    def _():
        for gi in range(rows // rg):
            _gla_state_store(gfin_ref, gst_ref.at[gi], gi * nbg, nbg)


def _mixer_fused_call(x3, s0, g0, w, gm, *, nb, seq, rg):
    b, t, d = x3.shape
    rows = nb * seq
    n_sub = MIXER_CHUNKS_PER_STEP
    assert b == nb and t % (seq * n_sub) == 0 and rows % rg == 0
    s5w = w["dskip"].shape[1]
    vw = g0.shape[1] * g0.shape[3]
    tok = lambda cdim: pl.BlockSpec((nb, seq * n_sub, cdim), lambda i: (0, i, 0))
    full = lambda arr: pl.BlockSpec(arr.shape, lambda i, nd=arr.ndim: (0,) * nd)
    consts = tuple(w[k] for k in PRE_CONSTS) + tuple(gm[k] for k in GLA_CONSTS)
    ns = (rg // seq) * GLA_DK
    return pl.pallas_call(
        functools.partial(_mixer_fused_body, nb=nb, seq=seq, rg=rg, n_sub=n_sub),
        grid=(t // (seq * n_sub),),
        in_specs=[tok(d), full(s0), full(g0)] + [full(a) for a in consts],
        out_specs=[tok(s5w), tok(vw), full(s0), full(g0)],
        out_shape=[jax.ShapeDtypeStruct((b, t, s5w), BF16), jax.ShapeDtypeStruct((b, t, vw), BF16),
                   jax.ShapeDtypeStruct(s0.shape, F32), jax.ShapeDtypeStruct(g0.shape, F32)],
        scratch_shapes=[pltpu.VMEM(s0.shape, F32),
                        pltpu.VMEM((w["bbig"].shape[0], rows, w["bbig"].shape[2]), F32),
                        pltpu.VMEM((rows // rg, GLA_HEADS, ns, GLA_DV), F32)],
        compiler_params=_params(("arbitrary",)),
        name="mixer",
    )(x3, s0, g0, *consts)


def _mixer_pre_body(*refs, nb, seq):
    n_pre = len(PRE_CONSTS)
    x_ref, s0_ref = refs[:2]
    c = dict(zip(PRE_CONSTS, refs[2:2 + n_pre]))
    zz_ref, qs_ref, kin_ref, kend_ref, v_ref, gs_ref, ef_ref, sfin_ref, st_ref, xs_ref = refs[2 + n_pre:]

    @pl.when(pl.program_id(0) == 0)
    def _():
        st_ref[...] = s0_ref[...]

    x = x_ref[...].reshape(nb * seq, x_ref.shape[-1])
    zz, qs, kin, kend, v, gs, ef = _pre_compute(x, st_ref, xs_ref, c, nb=nb, seq=seq)
    zz_ref[...] = zz.reshape(zz_ref.shape)
    sfin_ref[...] = st_ref[...]
    qs_ref[...] = qs.astype(BF16).reshape(qs_ref.shape)
    kin_ref[...] = kin.astype(BF16).reshape(kin_ref.shape)
    kend_ref[...] = kend.astype(BF16).reshape(kend_ref.shape)
    v_ref[...] = v.astype(BF16).reshape(v_ref.shape)
    gs_ref[...] = gs.reshape(gs_ref.shape)
    ef_ref[...] = ef.reshape(ef_ref.shape)


def _mixer_pre_call(x3, s0, w, *, nb, seq, blk):
    bv, tv, d = x3.shape
    bb, tt = blk
    rows = nb * seq
    assert bb * tt == rows and bv == bb and tv % tt == 0
    s5w = w["dskip"].shape[1]
    qkw = w["gate_b"].shape[1]
    vw = (w["w_in"].shape[1] - s5w - 2 * qkw) // 2
    tok = lambda cdim: pl.BlockSpec((bb, tt, cdim), lambda i: (0, i, 0))
    full = lambda arr: pl.BlockSpec(arr.shape, lambda i, nd=arr.ndim: (0,) * nd)
    consts = tuple(w[k] for k in PRE_CONSTS)
    tok_shape = lambda cdim, dt: jax.ShapeDtypeStruct((bv, tv, cdim), dt)
    return pl.pallas_call(
        functools.partial(_mixer_pre_body, nb=nb, seq=seq),
        grid=(tv // tt,),
        in_specs=[tok(d), full(s0)] + [full(a) for a in consts],
        out_specs=[tok(s5w), tok(qkw), tok(qkw), tok(qkw), tok(vw), tok(vw), tok(qkw), full(s0)],
        out_shape=[tok_shape(s5w, BF16), tok_shape(qkw, BF16), tok_shape(qkw, BF16), tok_shape(qkw, BF16),
                   tok_shape(vw, BF16), tok_shape(vw, F32), tok_shape(qkw, F32),
                   jax.ShapeDtypeStruct(s0.shape, F32)],
        scratch_shapes=[pltpu.VMEM(s0.shape, F32),
                        pltpu.VMEM((w["bbig"].shape[0], rows, w["bbig"].shape[2]), F32)],
        compiler_params=_params(("arbitrary",)),
        name="mixer_pre",
    )(x3, s0, *consts)


def _gla_body(*refs, rg, seq):
    qs_ref, kin_ref, kend_ref, v_ref, gs_ref, ef_ref, s0_ref = refs[:7]
    c = dict(zip(GLA_CONSTS, refs[7:7 + len(GLA_CONSTS)]))
    o_ref, sout_ref, s_ref = refs[7 + len(GLA_CONSTS):]
    nbg = rg // seq
    chunk = pl.program_id(1)

    @pl.when(chunk == 0)
    def _():
        _gla_state_load(s_ref, s0_ref, 0, nbg)

    flat = lambda ref: ref[...].reshape(rg, ref.shape[-1])
    heads = []
    for thunk in _gla_group_heads(flat(qs_ref).astype(F32), flat(kin_ref), flat(kend_ref).astype(F32), flat(v_ref),
                                  flat(gs_ref), flat(ef_ref), s_ref, c, heads, rg=rg, seq=seq):
        thunk()
    o = jnp.concatenate(heads, axis=1)
    o_ref[...] = o.reshape(o_ref.shape)

    @pl.when(chunk == pl.num_programs(1) - 1)
    def _():
        _gla_state_store(sout_ref, s_ref, 0, nbg)


def _gla_masks(rg, seq):
    nbg = rg // seq
    ns = nbg * GLA_DK
    tok = np.arange(rg)
    tb, tt = tok // seq, tok % seq
    srow = np.arange(ns)
    sb = srow // GLA_DK
    m1 = (tb[:, None] == tb[None, :]) & (tt[None, :] <= tt[:, None])
    m2 = tb[:, None] == sb[None, :]
    m3 = (sb[:, None] * seq) == tok[None, :]
    as_f32 = lambda a: jnp.asarray(a.astype(np.float32))
    return dict(m1=as_f32(m1), m2=as_f32(m2), m2t=as_f32(m2.T), m3=as_f32(m3))


def _gla_call(qs, kin, kend, v, gs, ef, s0, gm, *, rg, seq, blk, grid, tok_index, state_index):
    bb, tt = blk
    assert bb * tt == rg and rg % seq == 0
    nbg = rg // seq
    ns = nbg * GLA_DK
    consts = tuple(gm[k] for k in GLA_CONSTS)
    tok = lambda arr: pl.BlockSpec((bb, tt, arr.shape[2]), tok_index)
    full = lambda arr: pl.BlockSpec(arr.shape, lambda g, c, nd=arr.ndim: (0,) * nd)
    st = pl.BlockSpec((nbg,) + s0.shape[1:], state_index)
    return pl.pallas_call(
        functools.partial(_gla_body, rg=rg, seq=seq),
        grid=grid,
        in_specs=[tok(qs), tok(kin), tok(kend), tok(v), tok(gs), tok(ef), st] + [full(a) for a in consts],
        out_specs=[tok(v), st],
        out_shape=[jax.ShapeDtypeStruct(v.shape, BF16), jax.ShapeDtypeStruct(s0.shape, F32)],
        scratch_shapes=[pltpu.VMEM((GLA_HEADS, ns, GLA_DV), F32)],
        compiler_params=_params(("parallel", "arbitrary")),
        name="gla",
    )(qs, kin, kend, v, gs, ef, s0, *consts)


def _s5_discretise(lam_re, lam_im, log_dt, b_re, b_im):
    dt = jnp.exp(log_dt.astype(F32))[:, None]
    lr = lam_re.astype(F32)
    li = lam_im.astype(F32)
    mag = jnp.exp(lr * dt)
    abr = mag * jnp.cos(li * dt)
    abi = mag * jnp.sin(li * dt)
    den = lr * lr + li * li
    fr = ((abr - 1.0) * lr + abi * li) / den
    fi = (abi * lr - (abr - 1.0) * li) / den
    br = b_re.astype(F32)
    bi = b_im.astype(F32)
    bbr = fr[..., None] * br - fi[..., None] * bi
    bbi = fr[..., None] * bi + fi[..., None] * br
    return abr, abi, bbr, bbi


def _s5_maps(abr, abi, bbr, bbi, c_re, c_im, n_blk):
    groups, states, ch = bbr.shape
    gpb = groups // n_blk
    eye = jnp.eye(gpb, dtype=F32)

    def b_map(bb):
        return jnp.einsum("gph,gk->ghkp", bb, eye).reshape(gpb * ch, gpb * states)

    def c_map(cc):
        return jnp.einsum("ghp,gk->gpkh", cc, eye).reshape(gpb * states, gpb * ch)

    bbig, cbig, ar, ai = [], [], [], []
    for m in range(n_blk):
        gsl = slice(gpb * m, gpb * (m + 1))
        bbig.append(jnp.concatenate([b_map(bbr[gsl]), b_map(bbi[gsl])], axis=1))
        cbig.append(jnp.concatenate([c_map(c_re[gsl].astype(F32)), -c_map(c_im[gsl].astype(F32))], axis=0))
        ar.append(abr[gsl].reshape(-1))
        ai.append(abi[gsl].reshape(-1))
    return (jnp.stack(bbig).astype(BF16), jnp.stack(cbig).astype(BF16), jnp.stack(ar), jnp.stack(ai))


def _s5_state_pack(re, im, n_blk):
    b = re.shape[0]
    re = re.astype(F32).reshape(b, n_blk, -1)
    im = im.astype(F32).reshape(b, n_blk, -1)
    return jnp.stack([re, im], axis=2).reshape(b, -1)


def _s5_state_unpack(st, n_blk, groups, states):
    b = st.shape[0]
    st = st.reshape(b, n_blk, 2, -1)
    return st[:, :, 0].reshape(b, groups, states), st[:, :, 1].reshape(b, groups, states)


def _perm_matrix(nb, seq):
    rows = nb * seq
    dst = np.arange(rows)
    src = (dst % nb) * seq + dst // nb
    p = np.zeros((rows, rows), np.float32)
    p[dst, src] = 1.0
    return jnp.asarray(p, dtype=BF16), jnp.asarray(p.T, dtype=BF16)


def _cumsum_masks(grp, seq):
    tok = np.arange(grp)
    tb, tt = tok // seq, tok % seq
    same = tb[:, None] == tb[None, :]
    m = same & (tt[None, :] <= tt[:, None])
    if seq % SUBLANES:
        m = np.concatenate([m, same], axis=0)
    return jnp.asarray(m.astype(np.float32), dtype=BF16)


def _layer_weights(l, norm_ffn1, ffn1_gate, ffn1_up, ffn1_down, norm_mix, w_in, s5_lam_re, s5_lam_im, s5_log_dt,
                   s5_b_re, s5_b_im, s5_c_re, s5_c_im, s5_d, s5_glu_w, s5_glu_b, gla_gate_w, gla_gate_b, gla_norm,
                   w_out, norm_ffn2, ffn2_gate, ffn2_up, ffn2_down):
    d = w_in.shape[1]
    s5w = s5_d.shape[1]
    qkw = gla_gate_b.shape[1]
    main = w_in.shape[2] - GLA_GATE_RANK
    n_blk = 2
    abr, abi, bbr, bbi = _s5_discretise(s5_lam_re[l], s5_lam_im[l], s5_log_dt[l], s5_b_re[l], s5_b_im[l])
    bbig, cbig, ar, ai = _s5_maps(abr, abi, bbr, bbi, s5_c_re[l], s5_c_im[l], n_blk)
    w_a = jnp.zeros((d, LANES), F32).at[:, :GLA_GATE_RANK].set(w_in[l][:, main:])
    gate_w = jnp.zeros((LANES, qkw), F32).at[:GLA_GATE_RANK].set(gla_gate_w[l])
    row = lambda a: a.astype(F32).reshape(1, -1)
    return dict(
        norm_ffn1=row(norm_ffn1[l]), ffn1=(ffn1_gate[l].astype(BF16), ffn1_up[l].astype(BF16), ffn1_down[l].astype(BF16)),
        norm_ffn2=row(norm_ffn2[l]), ffn2=(ffn2_gate[l].astype(BF16), ffn2_up[l].astype(BF16), ffn2_down[l].astype(BF16)),
        norm_mix=row(norm_mix[l]), w_in=w_in[l][:, :main].astype(BF16), w_a=w_a.astype(BF16),
        bbig=bbig, cbig=cbig, abar_re=ar, abar_im=ai, dskip=row(s5_d[l]),
        glu_w=s5_glu_w[l].astype(BF16), glu_b=row(s5_glu_b[l]),
        gate_w=gate_w.astype(BF16), gate_b=row(gla_gate_b[l]), gla_norm=row(gla_norm[l]),
        w_out=w_out[l].astype(BF16).reshape(2, s5w, -1), n_blk=n_blk)


def _decoder(x, s5_re0, s5_im0, gla0, layers, norm_final, *, prompt):
    b, t, d = x.shape
    seq = (GLA_CHUNK if t % GLA_CHUNK == 0 else t) if prompt else t
    assert b * seq == ROW_TILE and b % SUBLANES == 0
    rg = 256 if prompt else 128
    n = b * t
    perm, perm_t = _perm_matrix(b, seq)
    mcs = _cumsum_masks(256, seq)
    gmasks = _gla_masks(rg, seq)
    gf = norm_final.astype(F32).reshape(1, -1)
    xf = x.astype(F32).reshape(n, d)
    new_re, new_im, new_gla = [], [], []
    for l, w in enumerate(layers):
        w = dict(w, perm=perm, perm_t=perm_t, mcs=mcs)
        gm = dict(gmasks, gla_norm=w["gla_norm"])
        groups, states = s5_re0.shape[2], s5_re0.shape[3]
        x1 = _ffn_call(xf, w["norm_ffn1"], *w["ffn1"], gf, final_norm=False)
        s0 = _s5_state_pack(s5_re0[l], s5_im0[l], w["n_blk"])
        g0 = gla0[l].astype(F32)
        if prompt:
            zz, o, sfin, s_new = _mixer_fused_call(x1.reshape(b, t, d), s0, g0, w, gm, nb=b, seq=seq, rg=rg)
        else:
            zz, qs, kin, kend, v, gs, ef, sfin = _mixer_pre_call(
                x1.reshape(1, n, d), s0, w, nb=b, seq=seq, blk=(1, n))
            o, s_new = _gla_call(qs, kin, kend, v, gs, ef, g0, gm, rg=rg, seq=seq, blk=(1, rg), grid=(n // rg, 1),
                                 tok_index=lambda g, c: (0, g, 0), state_index=lambda g, c: (g, 0, 0, 0))
        last = l == len(layers) - 1
        xf = _ffn_call(x1, w["norm_ffn2"], *w["ffn2"], gf, final_norm=last,
                       mix=(zz.reshape(n, -1), o.reshape(n, -1), w["w_out"]))
        hr, hi = _s5_state_unpack(sfin, w["n_blk"], groups, states)
        new_re.append(hr)
        new_im.append(hi)
        new_gla.append(s_new)
    return xf.reshape(b, t, d), jnp.stack(new_re), jnp.stack(new_im), jnp.stack(new_gla)


def kernel(x_prompt, x_sample, state_s5_re, state_s5_im, state_gla, norm_ffn1, ffn1_gate, ffn1_up, ffn1_down, norm_mix, w_in, s5_lam_re, s5_lam_im, s5_log_dt, s5_b_re, s5_b_im, s5_c_re, s5_c_im, s5_d, s5_glu_w, s5_glu_b, gla_gate_w, gla_gate_b, gla_norm, w_out, norm_ffn2, ffn2_gate, ffn2_up, ffn2_down, norm_final):
    depth = norm_ffn1.shape[0]
    layers = [_layer_weights(l, norm_ffn1, ffn1_gate, ffn1_up, ffn1_down, norm_mix, w_in, s5_lam_re, s5_lam_im,
                             s5_log_dt, s5_b_re, s5_b_im, s5_c_re, s5_c_im, s5_d, s5_glu_w, s5_glu_b, gla_gate_w,
                             gla_gate_b, gla_norm, w_out, norm_ffn2, ffn2_gate, ffn2_up, ffn2_down)
              for l in range(depth)]
    bp = x_prompt.shape[0]
    zero_s5 = jnp.zeros((depth, bp) + state_s5_re.shape[2:], F32)
    zero_gla = jnp.zeros((depth, bp) + state_gla.shape[2:], F32)
    y_p, p_re, p_im, p_gla = _decoder(x_prompt, zero_s5, zero_s5, zero_gla, layers, norm_final, prompt=True)
    y_s, s_re, s_im, s_gla = _decoder(x_sample, state_s5_re, state_s5_im, state_gla, layers, norm_final, prompt=False)
    return (y_p, y_s, p_re, p_im, p_gla, s_re, s_im, s_gla)
```
